```python
import math
import jax
import jax.numpy as jnp
from jax import lax
import numpy as np

D_MODEL = 1024
BATCH = 2
SEQ = 8192
DEPTH = 4
DEC_BATCH = 128
DEC_SEQ = 8
PAST_LEN = 2048
PAGE_SIZE = 128

N_MIXERS = 3
N_A_LAYERS = (DEPTH + 2) // 3
N_B_LAYERS = (DEPTH + 1) // 3
N_C_LAYERS = DEPTH // 3
DEEPNORM_ALPHA = (2 * DEPTH) ** 0.25
DEEPNORM_BETA = (8 * DEPTH) ** -0.25
LN_EPS = 1e-5
RMS_EPS = 1e-5
FFN_HALF = 0.5

D_FF = 2816

CMLP_CHUNK = 128
CMLP_DV = D_MODEL
CMLP_GROUPS = 8
CMLP_GDIM = CMLP_DV // CMLP_GROUPS

MOBA_HEAD_DIM = 128
MOBA_HEADS = D_MODEL // MOBA_HEAD_DIM
MOBA_BLOCK = 256
MOBA_TOPK = 3
MOBA_QBLOCK = 64
ROPE_THETA = 10000.0

SSM_D_INNER = 2 * D_MODEL
SSM_HEAD_DIM = 64
SSM_HEADS = SSM_D_INNER // SSM_HEAD_DIM
SSM_GROUPS = 8
SSM_HPG = SSM_HEADS // SSM_GROUPS
SSM_D_STATE = 128
SSM_CONV = 4
SSM_CONV_DIM = SSM_D_INNER + 2 * SSM_GROUPS * SSM_D_STATE
SSM_IN_DIM = SSM_D_INNER + SSM_CONV_DIM + SSM_HEADS
SSM_CHUNK = 128

kernel_name = "hybrid_gmlp_moba_ssd_macaron_deepnorm_step"


def layer_norm(x, g, b):
    xf = x.astype(jnp.float32)
    mu = jnp.mean(xf, axis=-1, keepdims=True)
    var = jnp.mean(jnp.square(xf - mu), axis=-1, keepdims=True)
    return ((xf - mu) * lax.rsqrt(var + LN_EPS) * g + b).astype(x.dtype)


def swiglu(x, w_in, w_out):
    gate, up = jnp.split(x @ w_in, 2, axis=-1)
    return (jax.nn.silu(gate) * up) @ w_out


def ffn_sublayer(h, w_in, w_out, g, b):
    return layer_norm(DEEPNORM_ALPHA * h + FFN_HALF * swiglu(h, w_in, w_out), g, b)


def rope(x, pos):
    half = MOBA_HEAD_DIM // 2
    inv = ROPE_THETA ** (-jnp.arange(half, dtype=jnp.float32) / half)
    ang = pos.astype(jnp.float32)[:, None] * inv[None, :]
    cos = jnp.cos(ang)[:, None, :]
    sin = jnp.sin(ang)[:, None, :]
    xf = x.astype(jnp.float32)
    x1, x2 = xf[..., :half], xf[..., half:]
    return jnp.concatenate([x1 * cos - x2 * sin, x2 * cos + x1 * sin], axis=-1).astype(x.dtype)


def chunk_mlp_mixer(x, w_in, ln_g, ln_b, w_s, b_s, w_out):
    bn, t, _ = x.shape
    u, v = jnp.split(jax.nn.gelu(x @ w_in, approximate=False), 2, axis=-1)
    v = layer_norm(v, ln_g, ln_b)
    lc = min(t, CMLP_CHUNK)
    causal = jnp.tril(jnp.ones((lc, lc), dtype=bool))
    w = jnp.where(causal, w_s[:, :lc, :lc], 0)
    vc = v.reshape(bn, t // lc, lc, CMLP_GROUPS, CMLP_GDIM)
    s = jnp.einsum("gts,bcsgd->bctgd", w, vc) + b_s[:, :lc].T[None, None, :, :, None]
    return (u * s.reshape(bn, t, CMLP_DV)) @ w_out, v


def moba_qkv(x, pos, w_qkv):
    bn, t, _ = x.shape
    qkv = (x @ w_qkv).reshape(bn, t, 3, MOBA_HEADS, MOBA_HEAD_DIM)
    return rope(qkv[:, :, 0], pos), rope(qkv[:, :, 1], pos), qkv[:, :, 2]


def moba_sequence(q, k, v, q_offset):
    tq = q.shape[0]
    n_keys = k.shape[0]
    nb = -(-n_keys // MOBA_BLOCK)
    pad = nb * MOBA_BLOCK - n_keys
    kb = jnp.pad(k, ((0, pad), (0, 0), (0, 0))).reshape(nb, MOBA_BLOCK, MOBA_HEADS, MOBA_HEAD_DIM).transpose(2, 0, 1, 3)
    vb = jnp.pad(v, ((0, pad), (0, 0), (0, 0))).reshape(nb, MOBA_BLOCK, MOBA_HEADS, MOBA_HEAD_DIM).transpose(2, 0, 1, 3)
    kmean = jnp.mean(kb.astype(jnp.float32), axis=2)
    topk = min(MOBA_TOPK, nb)
    qb = math.gcd(tq, MOBA_QBLOCK)
    scale = MOBA_HEAD_DIM ** -0.5
    head_ix = jnp.arange(MOBA_HEADS)[None, :, None]
    blk_ix = jnp.arange(nb)
    row_ix = jnp.arange(MOBA_BLOCK)

    def query_block(i):
        start = q_offset + i * qb
        q_blk = lax.dynamic_slice_in_dim(q, i * qb, qb, axis=0)
        qpos = start + jnp.arange(qb)
        own = start // MOBA_BLOCK
        gate = jnp.einsum("thd,hnd->thn", q_blk.astype(jnp.float32), kmean)
        gate = jnp.where(blk_ix < own, gate, -jnp.inf)
        gval, gidx = lax.top_k(gate, topk)
        sel_ok = jnp.isfinite(gval)
        k_own = lax.dynamic_index_in_dim(kb, own, axis=1, keepdims=False)
        v_own = lax.dynamic_index_in_dim(vb, own, axis=1, keepdims=False)
        s_own = jnp.einsum("thd,hrd->thr", q_blk, k_own).astype(jnp.float32) * scale
        s_own = jnp.where((own * MOBA_BLOCK + row_ix)[None, None, :] <= qpos[:, None, None], s_own, -jnp.inf)
        k_sel = kb[head_ix, gidx]
        v_sel = vb[head_ix, gidx]
        s_sel = jnp.einsum("thd,thjrd->thjr", q_blk, k_sel).astype(jnp.float32) * scale
        s_sel = jnp.where(sel_ok[..., None], s_sel, -jnp.inf)
        scores = jnp.concatenate([s_own, s_sel.reshape(qb, MOBA_HEADS, topk * MOBA_BLOCK)], axis=-1)
        p = jax.nn.softmax(scores, axis=-1).astype(v.dtype)
        p_own = p[..., :MOBA_BLOCK]
        p_sel = p[..., MOBA_BLOCK:].reshape(qb, MOBA_HEADS, topk, MOBA_BLOCK)
        return jnp.einsum("thr,hrd->thd", p_own, v_own) + jnp.einsum("thjr,thjrd->thd", p_sel, v_sel)

    out = lax.map(query_block, jnp.arange(tq // qb))
    return out.reshape(tq, MOBA_HEADS, MOBA_HEAD_DIM)


def moba_prompt_attend(q, k, v):
    return lax.map(lambda a: moba_sequence(a[0], a[1], a[2], 0), (q, k, v))


def moba_sample_attend(q, k_new, v_new, ck, cv, page_table):
    def one_sequence(a):
        q_s, k_s, v_s, pages = a
        k_past = ck[pages].reshape(-1, MOBA_HEADS, MOBA_HEAD_DIM)
        v_past = cv[pages].reshape(-1, MOBA_HEADS, MOBA_HEAD_DIM)
        return moba_sequence(q_s, jnp.concatenate([k_past, k_s], axis=0),
                             jnp.concatenate([v_past, v_s], axis=0), k_past.shape[0])
    return lax.map(one_sequence, (q, k_new, v_new, page_table))


def segsum(a):
    t = a.shape[-1]
    x = jnp.broadcast_to(a[..., :, None], a.shape + (t,))
    x = jnp.where(jnp.tril(jnp.ones((t, t), dtype=bool), -1), x, 0.0)
    cs = jnp.cumsum(x, axis=-2)
    return jnp.where(jnp.tril(jnp.ones((t, t), dtype=bool)), cs, -jnp.inf)


def ssd_scan(x, dt, a, bm, cm, init_state):
    bn, t = x.shape[:2]
    lc = math.gcd(t, SSM_CHUNK)
    nc = t // lc
    f32 = jnp.float32
    xdt = (x.astype(f32) * dt[..., None]).reshape(bn, nc, lc, SSM_GROUPS, SSM_HPG, SSM_HEAD_DIM)
    adt = (dt * a).reshape(bn, nc, lc, SSM_GROUPS, SSM_HPG).transpose(0, 1, 3, 4, 2)
    bc = bm.astype(f32).reshape(bn, nc, lc, SSM_GROUPS, SSM_D_STATE)
    cc = cm.astype(f32).reshape(bn, nc, lc, SSM_GROUPS, SSM_D_STATE)
    a_cs = jnp.cumsum(adt, axis=-1)
    lmat = jnp.exp(segsum(adt))
    cb = jnp.einsum("bclgn,bcsgn->bcgls", cc, bc)
    y_diag = jnp.einsum("bcgls,bcgels,bcsgep->bclgep", cb, lmat, xdt)
    decay_states = jnp.exp(a_cs[..., -1:] - a_cs)
    states = jnp.einsum("bclgn,bcgel,bclgep->bcgepn", bc, decay_states, xdt)
    init = init_state.astype(f32).reshape(bn, 1, SSM_GROUPS, SSM_HPG, SSM_HEAD_DIM, SSM_D_STATE)
    states = jnp.concatenate([init, states], axis=1)
    chunk_tot = jnp.pad(a_cs[..., -1], ((0, 0), (1, 0), (0, 0), (0, 0))).transpose(0, 2, 3, 1)
    decay_chunk = jnp.exp(segsum(chunk_tot))
    new_states = jnp.einsum("bgezc,bcgepn->bzgepn", decay_chunk, states)
    prev_states, final = new_states[:, :-1], new_states[:, -1]
    y_off = jnp.einsum("bclgn,bcgepn,bcgel->bclgep", cc, prev_states, jnp.exp(a_cs))
    y = (y_diag + y_off).reshape(bn, t, SSM_HEADS, SSM_HEAD_DIM)
    return y, final.reshape(bn, SSM_HEADS, SSM_HEAD_DIM, SSM_D_STATE)


def ssm_mixer(x, conv_state, ssm_state, w_in, w_conv, b_conv, dt_bias, a_log, d_skip, norm_g, w_out):
    bn, t, _ = x.shape
    zxbcdt = x @ w_in
    z = zxbcdt[..., :SSM_D_INNER]
    xbc_raw = zxbcdt[..., SSM_D_INNER:SSM_D_INNER + SSM_CONV_DIM]
    dt_raw = zxbcdt[..., SSM_D_INNER + SSM_CONV_DIM:]
    xp = jnp.concatenate([conv_state.astype(xbc_raw.dtype), xbc_raw], axis=1)
    conv = lax.conv_general_dilated(xp, w_conv[:, None, :].astype(xp.dtype), window_strides=(1,), padding="VALID",
                                    dimension_numbers=("NWC", "WIO", "NWC"), feature_group_count=SSM_CONV_DIM)
    xbc = jax.nn.silu(conv + b_conv)
    new_conv = xp[:, -(SSM_CONV - 1):]
    xs = xbc[..., :SSM_D_INNER].reshape(bn, t, SSM_HEADS, SSM_HEAD_DIM)
    bm = xbc[..., SSM_D_INNER:SSM_D_INNER + SSM_GROUPS * SSM_D_STATE].reshape(bn, t, SSM_GROUPS, SSM_D_STATE)
    cm = xbc[..., SSM_D_INNER + SSM_GROUPS * SSM_D_STATE:].reshape(bn, t, SSM_GROUPS, SSM_D_STATE)
    dt = jax.nn.softplus(dt_raw.astype(jnp.float32) + dt_bias.astype(jnp.float32))
    a = -jnp.exp(a_log.astype(jnp.float32))
    y, new_ssm = ssd_scan(xs, dt, a, bm, cm, ssm_state)
    y = y + xs.astype(jnp.float32) * d_skip.astype(jnp.float32)[:, None]
    y = y.reshape(bn, t, SSM_D_INNER) * jax.nn.silu(z.astype(jnp.float32))
    yg = y.reshape(bn, t, SSM_GROUPS, SSM_D_INNER // SSM_GROUPS)
    yg = yg * lax.rsqrt(jnp.mean(yg * yg, axis=-1, keepdims=True) + RMS_EPS)
    y = (yg.reshape(bn, t, SSM_D_INNER) * norm_g).astype(x.dtype)
    return y @ w_out, new_conv, new_ssm.astype(ssm_state.dtype)


def setup_inputs(seed: int = 0) -> dict:
    key = jax.random.key(seed)
    ks = jax.random.split(key, 32)
    f32 = jnp.float32

    def nrm(k, shape, s):
        return jax.random.normal(k, shape, f32) * s

    n_pages = PAST_LEN // PAGE_SIZE
    n_used = DEC_BATCH * n_pages
    n_pool = n_used + (n_used + 3) // 4
    page_table = jax.random.permutation(ks[0], n_pool)[:n_used].reshape(DEC_BATCH, n_pages).astype(jnp.int32)
    dt0 = jnp.exp(jax.random.uniform(ks[1], (N_C_LAYERS, SSM_HEADS), f32, math.log(1e-3), math.log(1e-1)))
    return {
        "x_prompt": nrm(ks[2], (BATCH, SEQ, D_MODEL), 1.0),
        "x_sample": nrm(ks[3], (DEC_BATCH, DEC_SEQ, D_MODEL), 1.0),
        "cache_k": nrm(ks[4], (N_B_LAYERS, n_pool, PAGE_SIZE, MOBA_HEADS, MOBA_HEAD_DIM), 1.0),
        "cache_v": nrm(ks[5], (N_B_LAYERS, n_pool, PAGE_SIZE, MOBA_HEADS, MOBA_HEAD_DIM), 1.0),
        "state_conv": nrm(ks[6], (N_C_LAYERS, DEC_BATCH, SSM_CONV - 1, SSM_CONV_DIM), 1.0),
        "state_ssm": nrm(ks[7], (N_C_LAYERS, DEC_BATCH, SSM_HEADS, SSM_HEAD_DIM, SSM_D_STATE), 0.1),
        "page_table": page_table,
        "ln_g": 1.0 + nrm(ks[8], (DEPTH, 3, D_MODEL), 0.02),
        "ln_b": nrm(ks[9], (DEPTH, 3, D_MODEL), 0.02),
        "ffn_w_in": nrm(ks[10], (DEPTH, 2, D_MODEL, 2 * D_FF), D_MODEL ** -0.5),
        "ffn_w_out": nrm(ks[11], (DEPTH, 2, D_FF, D_MODEL), DEEPNORM_BETA * D_FF ** -0.5),
        "cmlp_w_in": nrm(ks[12], (N_A_LAYERS, D_MODEL, 2 * CMLP_DV), D_MODEL ** -0.5),
        "cmlp_ln_g": 1.0 + nrm(ks[13], (N_A_LAYERS, CMLP_DV), 0.02),
        "cmlp_ln_b": nrm(ks[14], (N_A_LAYERS, CMLP_DV), 0.02),
        "cmlp_w_s": nrm(ks[15], (N_A_LAYERS, CMLP_GROUPS, CMLP_CHUNK, CMLP_CHUNK), CMLP_CHUNK ** -0.5),
        "cmlp_b_s": 1.0 + nrm(ks[16], (N_A_LAYERS, CMLP_GROUPS, CMLP_CHUNK), 0.02),
        "cmlp_w_out": nrm(ks[17], (N_A_LAYERS, CMLP_DV, D_MODEL), DEEPNORM_BETA * CMLP_DV ** -0.5),
        "moba_w_qkv": nrm(ks[18], (N_B_LAYERS, D_MODEL, 3 * MOBA_HEADS * MOBA_HEAD_DIM), D_MODEL ** -0.5),
        "moba_w_out": nrm(ks[19], (N_B_LAYERS, MOBA_HEADS * MOBA_HEAD_DIM, D_MODEL), DEEPNORM_BETA * (MOBA_HEADS * MOBA_HEAD_DIM) ** -0.5),
        "ssm_w_in": nrm(ks[20], (N_C_LAYERS, D_MODEL, SSM_IN_DIM), D_MODEL ** -0.5),
        "ssm_w_conv": nrm(ks[21], (N_C_LAYERS, SSM_CONV, SSM_CONV_DIM), SSM_CONV ** -0.5),
        "ssm_b_conv": nrm(ks[22], (N_C_LAYERS, SSM_CONV_DIM), 0.02),
        "ssm_dt_bias": dt0 + jnp.log(-jnp.expm1(-dt0)),
        "ssm_a_log": jnp.log(jax.random.uniform(ks[23], (N_C_LAYERS, SSM_HEADS), f32, 1.0, 16.0)),
        "ssm_d": 1.0 + nrm(ks[24], (N_C_LAYERS, SSM_HEADS), 0.02),
        "ssm_norm_g": 1.0 + nrm(ks[25], (N_C_LAYERS, SSM_D_INNER), 0.02),
        "ssm_w_out": nrm(ks[26], (N_C_LAYERS, SSM_D_INNER, D_MODEL), DEEPNORM_BETA * SSM_D_INNER ** -0.5),
    }


def reference(x_prompt, x_sample, cache_k, cache_v, state_conv, state_ssm, page_table,
              ln_g, ln_b, ffn_w_in, ffn_w_out,
              cmlp_w_in, cmlp_ln_g, cmlp_ln_b, cmlp_w_s, cmlp_b_s, cmlp_w_out,
              moba_w_qkv, moba_w_out,
              ssm_w_in, ssm_w_conv, ssm_b_conv, ssm_dt_bias, ssm_a_log, ssm_d, ssm_norm_g, ssm_w_out):
    n_prompt, t_prompt, _ = x_prompt.shape
    n_sample, t_sample, _ = x_sample.shape
    past_len = page_table.shape[1] * PAGE_SIZE
    pos_prompt = jnp.arange(t_prompt)
    pos_sample = past_len + jnp.arange(t_sample)
    hp, hs = x_prompt, x_sample
    cmlp_v_sample = []
    k_prompt, v_prompt, k_sample, v_sample = [], [], [], []
    conv_prompt, ssm_prompt, conv_sample, ssm_sample = [], [], [], []
    for i in range(DEPTH):
        hp = ffn_sublayer(hp, ffn_w_in[i, 0], ffn_w_out[i, 0], ln_g[i, 0], ln_b[i, 0])
        hs = ffn_sublayer(hs, ffn_w_in[i, 0], ffn_w_out[i, 0], ln_g[i, 0], ln_b[i, 0])
        kind, j = i % N_MIXERS, i // N_MIXERS
        if kind == 0:
            a_par = (cmlp_w_in[j], cmlp_ln_g[j], cmlp_ln_b[j], cmlp_w_s[j], cmlp_b_s[j], cmlp_w_out[j])
            mix_p, _ = chunk_mlp_mixer(hp, *a_par)
            mix_s, v_rows = chunk_mlp_mixer(hs, *a_par)
            cmlp_v_sample.append(v_rows)
        elif kind == 1:
            q, k, v = moba_qkv(hp, pos_prompt, moba_w_qkv[j])
            mix_p = moba_prompt_attend(q, k, v).reshape(n_prompt, t_prompt, -1) @ moba_w_out[j]
            k_prompt.append(k)
            v_prompt.append(v)
            q, k, v = moba_qkv(hs, pos_sample, moba_w_qkv[j])
            mix_s = moba_sample_attend(q, k, v, cache_k[j], cache_v[j], page_table).reshape(n_sample, t_sample, -1) @ moba_w_out[j]
            k_sample.append(k)
            v_sample.append(v)
        else:
            c_par = (ssm_w_in[j], ssm_w_conv[j], ssm_b_conv[j], ssm_dt_bias[j], ssm_a_log[j], ssm_d[j], ssm_norm_g[j], ssm_w_out[j])
            zero_conv = jnp.zeros((n_prompt, SSM_CONV - 1, SSM_CONV_DIM), hp.dtype)
            zero_ssm = jnp.zeros((n_prompt, SSM_HEADS, SSM_HEAD_DIM, SSM_D_STATE), state_ssm.dtype)
            mix_p, cp, sp = ssm_mixer(hp, zero_conv, zero_ssm, *c_par)
            mix_s, cs, ss = ssm_mixer(hs, state_conv[j], state_ssm[j], *c_par)
            conv_prompt.append(cp)
            ssm_prompt.append(sp)
            conv_sample.append(cs)
            ssm_sample.append(ss)
        hp = layer_norm(DEEPNORM_ALPHA * hp + mix_p, ln_g[i, 1], ln_b[i, 1])
        hs = layer_norm(DEEPNORM_ALPHA * hs + mix_s, ln_g[i, 1], ln_b[i, 1])
        hp = ffn_sublayer(hp, ffn_w_in[i, 1], ffn_w_out[i, 1], ln_g[i, 2], ln_b[i, 2])
        hs = ffn_sublayer(hs, ffn_w_in[i, 1], ffn_w_out[i, 1], ln_g[i, 2], ln_b[i, 2])
    return (hp, hs, jnp.stack(cmlp_v_sample), jnp.stack(k_prompt), jnp.stack(v_prompt), jnp.stack(k_sample), jnp.stack(v_sample), jnp.stack(conv_prompt), jnp.stack(ssm_prompt), jnp.stack(conv_sample), jnp.stack(ssm_sample))
```

```python
import functools
import math

import jax
import jax.numpy as jnp
from jax import lax
from jax.experimental import pallas as pl
from jax.experimental.pallas import tpu as pltpu

F32 = jnp.float32
BF16 = jnp.bfloat16
HIGHEST = lax.Precision.HIGHEST

LANES = 128
SUBLANES = 8
VMEM_LIMIT_BYTES = 56 * 1024 * 1024

DEPTH = 4
D_MODEL = 1024
D_FF = 2816
DEEPNORM_ALPHA = (2 * DEPTH) ** 0.25
LN_EPS = 1e-5
RMS_EPS = 1e-5
FFN_HALF = 0.5

CMLP_CHUNK = 128
CMLP_GROUPS = 8
CMLP_DV = D_MODEL

HEAD_DIM = 128
HEADS = D_MODEL // HEAD_DIM
MOBA_BLOCK = 256
MOBA_TOPK = 3
PAGE_SIZE = 128
ROPE_THETA = 10000.0

SSM_D_INNER = 2 * D_MODEL
SSM_HEAD_DIM = 64
SSM_HEADS = SSM_D_INNER // SSM_HEAD_DIM
SSM_GROUPS = 8
SSM_HPG = SSM_HEADS // SSM_GROUPS
SSM_D_STATE = 128
SSM_CONV = 4
SSM_BC_DIM = SSM_GROUPS * SSM_D_STATE
SSM_CONV_DIM = SSM_D_INNER + 2 * SSM_BC_DIM
SSM_CHUNK = 128

NEG_INF = float("-inf")


def _params(semantics):
    return pltpu.CompilerParams(dimension_semantics=semantics, vmem_limit_bytes=VMEM_LIMIT_BYTES)


def _resident(shape):
    zeros = (0,) * len(shape)
    return pl.BlockSpec(shape, lambda *_: zeros, pipeline_mode=pl.Buffered(1))


def _layer_norm(y, g, b):
    mu = jnp.mean(y, axis=-1, keepdims=True)
    d = y - mu
    var = jnp.mean(d * d, axis=-1, keepdims=True)
    return d * lax.rsqrt(var + LN_EPS) * g + b


def _silu(x):
    return x * jax.nn.sigmoid(x)


FFN_COL_CHUNK = 256


def _ffn_kernel(x_ref, win_ref, wout_ref, g_ref, b_ref, o_ref, act_ref):
    x = x_ref[...]
    xb = x.astype(BF16)
    for c in range(D_FF // FFN_COL_CHUNK):
        lo = c * FFN_COL_CHUNK
        gate = jnp.dot(xb, win_ref[:, lo:lo + FFN_COL_CHUNK], preferred_element_type=F32)
        up = jnp.dot(xb, win_ref[:, D_FF + lo:D_FF + lo + FFN_COL_CHUNK], preferred_element_type=F32)
        act_ref[:, lo:lo + FFN_COL_CHUNK] = (_silu(gate) * up).astype(BF16)
    y = jnp.dot(act_ref[...], wout_ref[...], preferred_element_type=F32)
    o_ref[...] = _layer_norm(DEEPNORM_ALPHA * x + FFN_HALF * y, g_ref[...], b_ref[...])


def _ffn(h, w_in, w_out, g, b, *, tm=512):
    n = h.shape[0]
    return pl.pallas_call(
        _ffn_kernel,
        grid=(n // tm,),
        in_specs=[
            pl.BlockSpec((tm, D_MODEL), lambda i: (i, 0)),
            _resident((D_MODEL, 2 * D_FF)),
            _resident((D_FF, D_MODEL)),
            _resident((1, D_MODEL)),
            _resident((1, D_MODEL)),
        ],
        out_specs=pl.BlockSpec((tm, D_MODEL), lambda i: (i, 0)),
        out_shape=jax.ShapeDtypeStruct((n, D_MODEL), F32),
        scratch_shapes=[pltpu.VMEM((tm, D_FF), BF16)],
        compiler_params=_params(("parallel",)),
        name="ffn",
    )(h, w_in, w_out, g, b)


def _proj_ln_kernel(h_ref, pre_ref, w_ref, g_ref, b_ref, o_ref):
    y = jnp.dot(pre_ref[...], w_ref[...], preferred_element_type=F32)
    o_ref[...] = _layer_norm(DEEPNORM_ALPHA * h_ref[...] + y, g_ref[...], b_ref[...])


def _proj_ln(h, pre, w_out, g, b, *, tm=512):
    n = h.shape[0]
    k = pre.shape[1]
    return pl.pallas_call(
        _proj_ln_kernel,
        grid=(n // tm,),
        in_specs=[
            pl.BlockSpec((tm, D_MODEL), lambda i: (i, 0)),
            pl.BlockSpec((tm, k), lambda i: (i, 0)),
            _resident((k, D_MODEL)),
            _resident((1, D_MODEL)),
            _resident((1, D_MODEL)),
        ],
        out_specs=pl.BlockSpec((tm, D_MODEL), lambda i: (i, 0)),
        out_shape=jax.ShapeDtypeStruct((n, D_MODEL), F32),
        compiler_params=_params(("parallel",)),
        name="proj_ln",
    )(h, pre, w_out, g, b)


def _cmlp_kernel(h_ref, win_ref, lng_ref, lnb_ref, wsp_ref, bias_ref, wout_ref, g_ref, b_ref,
                 o_ref, v_ref, pre_ref):
    tm = h_ref.shape[0]
    x = h_ref[...]
    uv = jnp.dot(x.astype(BF16), win_ref[...], preferred_element_type=F32)
    uv = 0.5 * uv * (1.0 + lax.erf(uv * math.sqrt(0.5)))
    u = uv[:, :CMLP_DV]
    v = _layer_norm(uv[:, CMLP_DV:], lng_ref[...], lnb_ref[...])
    v_ref[...] = v
    vb = v.astype(BF16)
    row = lax.broadcasted_iota(jnp.int32, (CMLP_CHUNK, CMLP_CHUNK), 0)
    col = lax.broadcasted_iota(jnp.int32, (CMLP_CHUNK, CMLP_CHUNK), 1)
    causal = col <= row
    for g in range(CMLP_GROUPS):
        gl = g * LANES
        wg = jnp.where(causal, wsp_ref[g], 0.0).astype(BF16)
        for c in range(tm // CMLP_CHUNK):
            cl = c * CMLP_CHUNK
            s = jnp.dot(wg, vb[cl:cl + CMLP_CHUNK, gl:gl + LANES], preferred_element_type=F32)
            s = s + bias_ref[:, gl:gl + LANES]
            pre_ref[cl:cl + CMLP_CHUNK, gl:gl + LANES] = (u[cl:cl + CMLP_CHUNK, gl:gl + LANES] * s).astype(BF16)
    y = jnp.dot(pre_ref[...], wout_ref[...], preferred_element_type=F32)
    o_ref[...] = _layer_norm(DEEPNORM_ALPHA * x + y, g_ref[...], b_ref[...])


def _cmlp(h, w_in, ln_g, ln_b, w_sp, bias_full, w_out, g, b, *, tm=256):
    n = h.shape[0]
    return pl.pallas_call(
        _cmlp_kernel,
        grid=(n // tm,),
        in_specs=[
            pl.BlockSpec((tm, D_MODEL), lambda i: (i, 0)),
            _resident((D_MODEL, 2 * CMLP_DV)),
            _resident((1, CMLP_DV)),
            _resident((1, CMLP_DV)),
            _resident((CMLP_GROUPS, CMLP_CHUNK, CMLP_CHUNK)),
            _resident((CMLP_CHUNK, CMLP_DV)),
            _resident((CMLP_DV, D_MODEL)),
            _resident((1, D_MODEL)),
            _resident((1, D_MODEL)),
        ],
        out_specs=[
            pl.BlockSpec((tm, D_MODEL), lambda i: (i, 0)),
            pl.BlockSpec((tm, CMLP_DV), lambda i: (i, 0)),
        ],
        out_shape=[
            jax.ShapeDtypeStruct((n, D_MODEL), F32),
            jax.ShapeDtypeStruct((n, CMLP_DV), F32),
        ],
        scratch_shapes=[pltpu.VMEM((tm, CMLP_DV), BF16)],
        compiler_params=_params(("parallel",)),
        name="cmlp",
    )(h, w_in, ln_g, ln_b, w_sp, bias_full, w_out, g, b)


def _qkv_kernel(h_ref, w_ref, cos_ref, sin_ref, q_ref, k_ref, v_ref, kb_ref, vb_ref, km_ref):
    tm = h_ref.shape[0]
    qkv = jnp.dot(h_ref[...].astype(BF16), w_ref[...], preferred_element_type=F32)
    cos = cos_ref[...]
    sin = sin_ref[...]
    for hd in range(HEADS):
        lo = hd * HEAD_DIM
        qh = qkv[:, lo:lo + HEAD_DIM]
        q_ref[:, lo:lo + HEAD_DIM] = (qh * cos + pltpu.roll(qh, HEAD_DIM // 2, axis=1) * sin).astype(BF16)
        kh = qkv[:, D_MODEL + lo:D_MODEL + lo + HEAD_DIM]
        kh = kh * cos + pltpu.roll(kh, HEAD_DIM // 2, axis=1) * sin
        k_ref[:, lo:lo + HEAD_DIM] = kh
        kb_ref[:, lo:lo + HEAD_DIM] = kh.astype(BF16)
    v = qkv[:, 2 * D_MODEL:]
    v_ref[...] = v
    vb_ref[...] = v.astype(BF16)
    for j in range(tm // MOBA_BLOCK):
        blk = k_ref[j * MOBA_BLOCK:(j + 1) * MOBA_BLOCK, :]
        km_ref[0, j:j + 1, :] = jnp.sum(blk, axis=0, keepdims=True) * (1.0 / MOBA_BLOCK)


def _qkv(h, w_qkv, cos, sin, *, tm=512):
    n = h.shape[0]
    nkb = tm // MOBA_BLOCK
    row_spec = lambda width: pl.BlockSpec((tm, width), lambda i: (i, 0))
    return pl.pallas_call(
        _qkv_kernel,
        grid=(n // tm,),
        in_specs=[
            row_spec(D_MODEL),
            _resident((D_MODEL, 3 * D_MODEL)),
            row_spec(HEAD_DIM),
            row_spec(HEAD_DIM),
        ],
        out_specs=[row_spec(D_MODEL)] * 5 + [pl.BlockSpec((1, nkb, D_MODEL), lambda i: (i, 0, 0))],
        out_shape=[
            jax.ShapeDtypeStruct((n, D_MODEL), BF16),
            jax.ShapeDtypeStruct((n, D_MODEL), F32),
            jax.ShapeDtypeStruct((n, D_MODEL), F32),
            jax.ShapeDtypeStruct((n, D_MODEL), BF16),
            jax.ShapeDtypeStruct((n, D_MODEL), BF16),
            jax.ShapeDtypeStruct((n // tm, nkb, D_MODEL), F32),
        ],
        compiler_params=_params(("parallel",)),
        name="qkv_rope",
    )(h, w_qkv, cos, sin)


def _top3_lanes(gate, n_valid):
    lane = lax.broadcasted_iota(jnp.int32, gate.shape, 1)
    gate = jnp.where(lane < n_valid, gate, NEG_INF)
    bias = jnp.full(gate.shape, NEG_INF, F32)
    for _ in range(MOBA_TOPK):
        m = jnp.max(gate, axis=1, keepdims=True)
        first = jnp.min(jnp.where(gate == m, lane, LANES), axis=1, keepdims=True)
        pick = lane == first
        bias = jnp.where(pick & (m > NEG_INF), 0.0, bias)
        gate = jnp.where(pick, NEG_INF, gate)
    return bias


def _attn_prompt_kernel(q_ref, k_ref, v_ref, km_ref, o_ref, bias_ref):
    i = pl.program_id(2)
    tq = q_ref.shape[1]
    nb = km_ref.shape[1]
    scale = HEAD_DIM ** -0.5
    q = q_ref[0]
    km = jnp.concatenate([km_ref[0], jnp.zeros((LANES - nb, HEAD_DIM), F32)], axis=0)
    gate = lax.dot_general(q.astype(F32), km, (((1,), (1,)), ((), ())),
                           precision=HIGHEST, preferred_element_type=F32)
    bias_ref[...] = _top3_lanes(gate, i)
    lane = lax.broadcasted_iota(jnp.int32, (tq, LANES), 1)

    def scores(j):
        kj = k_ref[0, pl.ds(pl.multiple_of(j * MOBA_BLOCK, MOBA_BLOCK), MOBA_BLOCK), :]
        return lax.dot_general(q, kj, (((1,), (1,)), ((), ())), preferred_element_type=F32) * scale

    def values(j):
        return v_ref[0, pl.ds(pl.multiple_of(j * MOBA_BLOCK, MOBA_BLOCK), MOBA_BLOCK), :]

    row = lax.broadcasted_iota(jnp.int32, (tq, MOBA_BLOCK), 0)
    col = lax.broadcasted_iota(jnp.int32, (tq, MOBA_BLOCK), 1)
    s = jnp.where(col <= row, scores(i), NEG_INF)
    m0 = jnp.max(s, axis=1, keepdims=True)
    p = jnp.exp(s - m0)
    l0 = jnp.sum(p, axis=1, keepdims=True)
    acc0 = jnp.dot(p.astype(BF16), values(i), preferred_element_type=F32)

    def body(j, carry):
        m, l, acc = carry
        colbias = jnp.sum(jnp.where(lane == j, bias_ref[...], 0.0), axis=1, keepdims=True)
        s = scores(j) + colbias
        m_new = jnp.maximum(m, jnp.max(s, axis=1, keepdims=True))
        alpha = jnp.exp(m - m_new)
        p = jnp.exp(s - m_new)
        l = alpha * l + jnp.sum(p, axis=1, keepdims=True)
        acc = alpha * acc + jnp.dot(p.astype(BF16), values(j), preferred_element_type=F32)
        return m_new, l, acc

    _, l, acc = lax.fori_loop(0, i, body, (m0, l0, acc0))
    o_ref[0] = (acc / l).astype(BF16)


def _attn_prompt(q, k, v, kmean):
    bn, t, _ = q.shape
    nb = t // MOBA_BLOCK
    tq = MOBA_BLOCK
    return pl.pallas_call(
        _attn_prompt_kernel,
        grid=(bn, HEADS, t // tq),
        in_specs=[
            pl.BlockSpec((1, tq, HEAD_DIM), lambda b, h, i: (b, i, h)),
            pl.BlockSpec((1, t, HEAD_DIM), lambda b, h, i: (b, 0, h)),
            pl.BlockSpec((1, t, HEAD_DIM), lambda b, h, i: (b, 0, h)),
            pl.BlockSpec((1, nb, HEAD_DIM), lambda b, h, i: (b, 0, h)),
        ],
        out_specs=pl.BlockSpec((1, tq, HEAD_DIM), lambda b, h, i: (b, i, h)),
        out_shape=jax.ShapeDtypeStruct((bn, t, HEADS * HEAD_DIM), BF16),
        scratch_shapes=[pltpu.VMEM((tq, LANES), F32)],
        compiler_params=_params(("parallel", "parallel", "arbitrary")),
        name="moba_prompt",
    )(q, k, v, kmean)


def _attn_sample_kernel(pt_ref, q_ref, kn_ref, vn_ref, *rest, n_pages):
    kp_refs = rest[:n_pages]
    vp_refs = rest[n_pages:2 * n_pages]
    o_ref, kb_ref, vb_ref, s_ref = rest[2 * n_pages:]
    del pt_ref
    tq = q_ref.shape[1]
    past = n_pages * PAGE_SIZE
    n_past_blocks = past // MOBA_BLOCK
    pages_per_block = MOBA_BLOCK // PAGE_SIZE
    n_new = 2 * SUBLANES
    scale = HEAD_DIM ** -0.5

    q = q_ref[0].astype(F32)
    qt = jnp.concatenate([q] * (LANES // tq), axis=0)
    rid = lax.broadcasted_iota(jnp.int32, (LANES, D_MODEL), 0)
    lid = lax.broadcasted_iota(jnp.int32, (LANES, D_MODEL), 1)
    qr = jnp.where((rid < HEADS * tq) & (lid // HEAD_DIM == rid // tq), qt, 0.0).astype(BF16)

    kmean_rows = []
    for n in range(n_past_blocks):
        acc = jnp.zeros((1, D_MODEL), F32)
        for r in range(pages_per_block):
            p = n * pages_per_block + r
            kp = kp_refs[p][0]
            kb_ref[p * PAGE_SIZE:(p + 1) * PAGE_SIZE, :] = kp.astype(BF16)
            vb_ref[p * PAGE_SIZE:(p + 1) * PAGE_SIZE, :] = vp_refs[p][0].astype(BF16)
            acc = acc + jnp.sum(kp, axis=0, keepdims=True)
        kmean_rows.append(acc * (1.0 / MOBA_BLOCK))
    kmean = jnp.concatenate(kmean_rows + [jnp.zeros((LANES - n_past_blocks, D_MODEL), F32)], axis=0)
    zpad = jnp.zeros((n_new - tq, D_MODEL), F32)
    kb_ref[past:past + n_new, :] = jnp.concatenate([kn_ref[0], zpad], axis=0).astype(BF16)
    vb_ref[past:past + n_new, :] = jnp.concatenate([vn_ref[0], zpad], axis=0).astype(BF16)

    nt = (((1,), (1,)), ((), ()))
    gate = lax.dot_general(qr.astype(F32), kmean, nt, precision=HIGHEST, preferred_element_type=F32)
    bias = _top3_lanes(gate, n_past_blocks)

    s_ref[:, :past] = lax.dot_general(qr, kb_ref[:past, :], nt, preferred_element_type=F32) * scale
    s_new = lax.dot_general(qr, kb_ref[past:past + n_new, :], nt, preferred_element_type=F32) * scale
    qi = lax.broadcasted_iota(jnp.int32, (LANES, n_new), 0) % tq
    kr = lax.broadcasted_iota(jnp.int32, (LANES, n_new), 1)
    s_new = jnp.where(kr <= qi, s_new, NEG_INF)
    m = jnp.max(s_new, axis=1, keepdims=True)
    for n in range(n_past_blocks):
        lo = n * MOBA_BLOCK
        sb = s_ref[:, lo:lo + MOBA_BLOCK] + bias[:, n:n + 1]
        s_ref[:, lo:lo + MOBA_BLOCK] = sb
        m = jnp.maximum(m, jnp.max(sb, axis=1, keepdims=True))
    p_new = jnp.exp(s_new - m)
    l = jnp.sum(p_new, axis=1, keepdims=True)
    for n in range(n_past_blocks):
        lo = n * MOBA_BLOCK
        pb = jnp.exp(s_ref[:, lo:lo + MOBA_BLOCK] - m)
        s_ref[:, lo:lo + MOBA_BLOCK] = pb
        l = l + jnp.sum(pb, axis=1, keepdims=True)
    inv_l = 1.0 / l
    out = jnp.dot((s_ref[:, :past] * inv_l).astype(BF16), vb_ref[:past, :], preferred_element_type=F32)
    out = out + jnp.dot((p_new * inv_l).astype(BF16), vb_ref[past:past + n_new, :], preferred_element_type=F32)
    for hd in range(HEADS):
        lo = hd * HEAD_DIM
        o_ref[0, :, lo:lo + HEAD_DIM] = out[hd * tq:(hd + 1) * tq, lo:lo + HEAD_DIM].astype(BF16)


def _attn_sample(q, k_new, v_new, cache_k, cache_v, page_table):
    ns, tq, _ = q.shape
    n_pages = page_table.shape[1]
    past = n_pages * PAGE_SIZE
    n_new = 2 * SUBLANES
    assert tq == SUBLANES and HEADS * tq <= LANES and past % MOBA_BLOCK == 0

    def page_spec(p):
        return pl.BlockSpec((1, PAGE_SIZE, D_MODEL), lambda s, pt: (pt[s * n_pages + p], 0, 0))

    seq_spec = pl.BlockSpec((1, tq, D_MODEL), lambda s, pt: (s, 0, 0))
    grid_spec = pltpu.PrefetchScalarGridSpec(
        num_scalar_prefetch=1,
        grid=(ns,),
        in_specs=[seq_spec, seq_spec, seq_spec]
        + [page_spec(p) for p in range(n_pages)]
        + [page_spec(p) for p in range(n_pages)],
        out_specs=seq_spec,
        scratch_shapes=[
            pltpu.VMEM((past + n_new, D_MODEL), BF16),
            pltpu.VMEM((past + n_new, D_MODEL), BF16),
            pltpu.VMEM((LANES, past), F32),
        ],
    )
    return pl.pallas_call(
        functools.partial(_attn_sample_kernel, n_pages=n_pages),
        grid_spec=grid_spec,
        out_shape=jax.ShapeDtypeStruct((ns, tq, D_MODEL), BF16),
        compiler_params=_params(("arbitrary",)),
        name="moba_sample",
    )(page_table.reshape(-1), q, k_new, v_new, *([cache_k] * n_pages), *([cache_v] * n_pages))


def _ssm_in_kernel(h_ref, wz_ref, wx_ref, wdt_ref, z_ref, raw_ref, dt_ref):
    xb = h_ref[...].astype(BF16)
    z_ref[...] = jnp.dot(xb, wz_ref[...], preferred_element_type=F32)
    raw_ref[...] = jnp.dot(xb, wx_ref[...], preferred_element_type=F32)
    dt_ref[...] = jnp.dot(xb, wdt_ref[...], preferred_element_type=F32)


def _ssm_in(h, w_z, w_x, w_dt, *, tm=256):
    n = h.shape[0]
    row_spec = lambda width: pl.BlockSpec((tm, width), lambda i: (i, 0))
    return pl.pallas_call(
        _ssm_in_kernel,
        grid=(n // tm,),
        in_specs=[
            row_spec(D_MODEL),
            _resident((D_MODEL, SSM_D_INNER)),
            _resident((D_MODEL, SSM_CONV_DIM)),
            _resident((D_MODEL, LANES)),
        ],
        out_specs=[row_spec(SSM_D_INNER), row_spec(SSM_CONV_DIM), row_spec(LANES)],
        out_shape=[
            jax.ShapeDtypeStruct((n, SSM_D_INNER), F32),
            jax.ShapeDtypeStruct((n, SSM_CONV_DIM), F32),
            jax.ShapeDtypeStruct((n, LANES), F32),
        ],
        compiler_params=_params(("parallel",)),
        name="ssm_in",
    )(h, w_z, w_x, w_dt)


def _ssd_kernel(raw_ref, z_ref, dtr_ref, conv0_ref, ssm0_ref, wconv_ref, bconv_ref, dtb_ref, alog_ref,
                dskip_ref, ng_ref, tri_ref,
                y_ref, convo_ref, ssmo_ref, ext_ref, state_ref, yacc_ref):
    c = pl.program_id(1)
    t = raw_ref.shape[0]
    L = SSM_CHUNK
    halo = SUBLANES

    @pl.when(c == 0)
    def _():
        ext_ref[0:halo, :] = conv0_ref[0]
        state_ref[...] = ssm0_ref[0]

    ext_ref[halo:halo + t, :] = raw_ref[...]
    if t < L:
        ext_ref[halo + t:halo + L, :] = jnp.zeros((L - t, SSM_CONV_DIM), F32)
    conv = bconv_ref[...]
    for w in range(SSM_CONV):
        lo = halo - (SSM_CONV - 1) + w
        conv = conv + wconv_ref[w:w + 1, :] * ext_ref[lo:lo + L, :]
    tail = ext_ref[t:t + halo, :]
    ext_ref[0:halo, :] = tail
    convo_ref[0] = tail
    xbc = _silu(conv)
    x = xbc[:, :SSM_D_INNER]
    bm = xbc[:, SSM_D_INNER:SSM_D_INNER + SSM_BC_DIM].astype(BF16)
    cm = xbc[:, SSM_D_INNER + SSM_BC_DIM:]

    dtx = dtr_ref[...] + dtb_ref[...]
    dt = jnp.maximum(dtx, 0.0) + jnp.log1p(jnp.exp(-jnp.abs(dtx)))
    if t < L:
        dt = jnp.concatenate([dt, jnp.zeros((L - t, LANES), F32)], axis=0)
    a = -jnp.exp(alog_ref[...])
    adt = dt * a
    acs = jnp.dot(tri_ref[...], adt, precision=HIGHEST, preferred_element_type=F32)
    acs_t = acs.T
    dt_t = dt.T
    w_t = jnp.exp(acs_t[:, L - 1:L] - acs_t) * dt_t
    x_t = x.T
    xb = x.astype(BF16)

    row = lax.broadcasted_iota(jnp.int32, (L, L), 0)
    col = lax.broadcasted_iota(jnp.int32, (L, L), 1)
    causal = col <= row
    nt = (((1,), (1,)), ((), ()))
    for g in range(SSM_GROUPS):
        bg = bm[:, g * SSM_D_STATE:(g + 1) * SSM_D_STATE]
        cg = cm[:, g * SSM_D_STATE:(g + 1) * SSM_D_STATE]
        cb = lax.dot_general(cg.astype(BF16), bg, nt, preferred_element_type=F32)
        for e in range(SSM_HPG):
            hd = g * SSM_HPG + e
            lo = hd * SSM_HEAD_DIM
            acs_col = jnp.broadcast_to(acs[:, hd:hd + 1], (L, L))
            lmat = jnp.where(causal, jnp.exp(acs_col - acs_t[hd:hd + 1, :]), 0.0)
            mh = (cb * lmat * dt_t[hd:hd + 1, :]).astype(BF16)
            ch = (cg * jnp.exp(acs_col)).astype(BF16)
            sh = state_ref[lo:lo + SSM_HEAD_DIM, :]
            yh = jnp.dot(mh, xb[:, lo:lo + SSM_HEAD_DIM], preferred_element_type=F32)
            yh = yh + lax.dot_general(ch, sh.astype(BF16), nt, preferred_element_type=F32)
            yacc_ref[:, lo:lo + SSM_HEAD_DIM] = yh
            xw = (x_t[lo:lo + SSM_HEAD_DIM, :] * w_t[hd:hd + 1, :]).astype(BF16)
            upd = jnp.dot(xw, bg, preferred_element_type=F32)
            state_ref[lo:lo + SSM_HEAD_DIM, :] = jnp.exp(acs_t[hd:hd + 1, L - 1:L]) * sh + upd
    ssmo_ref[0] = state_ref[...]

    y = yacc_ref[0:t, :] + x[:t] * dskip_ref[...]
    y = y * _silu(z_ref[...])
    gw = SSM_D_INNER // SSM_GROUPS
    for g in range(SSM_GROUPS):
        yg = y[:, g * gw:(g + 1) * gw]
        yg = yg * lax.rsqrt(jnp.mean(yg * yg, axis=-1, keepdims=True) + RMS_EPS)
        y_ref[:, g * gw:(g + 1) * gw] = (yg * ng_ref[:, g * gw:(g + 1) * gw]).astype(y_ref.dtype)


def _ssd(raw, z, dt_raw, conv0, ssm0, w_conv, b_conv, dt_bias, a_log, d_skip, norm_g):
    bn, t_total, _ = raw.shape
    t = min(t_total, SSM_CHUNK)
    nc = t_total // t
    tri = jnp.tril(jnp.ones((SSM_CHUNK, SSM_CHUNK), F32))
    blk = lambda width: pl.BlockSpec((None, t, width), lambda b, c: (b, c, 0))
    per_seq = lambda rows, width: pl.BlockSpec((1, rows, width), lambda b, c: (b, 0, 0))
    return pl.pallas_call(
        _ssd_kernel,
        grid=(bn, nc),
        in_specs=[
            blk(SSM_CONV_DIM), blk(SSM_D_INNER), blk(LANES),
            per_seq(SUBLANES, SSM_CONV_DIM),
            per_seq(SSM_D_INNER, SSM_D_STATE),
            _resident((SUBLANES, SSM_CONV_DIM)),
            _resident((1, SSM_CONV_DIM)),
            _resident((1, LANES)),
            _resident((1, LANES)),
            _resident((1, SSM_D_INNER)),
            _resident((1, SSM_D_INNER)),
            _resident((SSM_CHUNK, SSM_CHUNK)),
        ],
        out_specs=[
            blk(SSM_D_INNER),
            per_seq(SUBLANES, SSM_CONV_DIM),
            per_seq(SSM_D_INNER, SSM_D_STATE),
        ],
        out_shape=[
            jax.ShapeDtypeStruct((bn, t_total, SSM_D_INNER), BF16),
            jax.ShapeDtypeStruct((bn, SUBLANES, SSM_CONV_DIM), F32),
            jax.ShapeDtypeStruct((bn, SSM_D_INNER, SSM_D_STATE), F32),
        ],
        scratch_shapes=[
            pltpu.VMEM((SUBLANES + SSM_CHUNK, SSM_CONV_DIM), F32),
            pltpu.VMEM((SSM_D_INNER, SSM_D_STATE), F32),
            pltpu.VMEM((SSM_CHUNK, SSM_D_INNER), F32),
        ],
        compiler_params=_params(("arbitrary", "arbitrary")),
        name="ssd",
    )(raw, z, dt_raw, conv0, ssm0, w_conv, b_conv, dt_bias, a_log, d_skip, norm_g, tri)


def _rope_tables(pos):
    half = HEAD_DIM // 2
    inv = ROPE_THETA ** (-jnp.arange(half, dtype=F32) / half)
    ang = pos.astype(F32)[:, None] * inv[None, :]
    cos, sin = jnp.cos(ang), jnp.sin(ang)
    return jnp.concatenate([cos, cos], axis=-1), jnp.concatenate([-sin, sin], axis=-1)


def _row(v):
    return v.reshape(1, -1)


def kernel(x_prompt, x_sample, cache_k, cache_v, state_conv, state_ssm, page_table, ln_g, ln_b, ffn_w_in, ffn_w_out, cmlp_w_in, cmlp_ln_g, cmlp_ln_b, cmlp_w_s, cmlp_b_s, cmlp_w_out, moba_w_qkv, moba_w_out, ssm_w_in, ssm_w_conv, ssm_b_conv, ssm_dt_bias, ssm_a_log, ssm_d, ssm_norm_g, ssm_w_out):
    n_prompt, t_prompt, _ = x_prompt.shape
    n_sample, t_sample, _ = x_sample.shape
    n_pages = page_table.shape[1]
    past_len = n_pages * PAGE_SIZE
    hp = x_prompt.reshape(n_prompt * t_prompt, D_MODEL)
    hs = x_sample.reshape(n_sample * t_sample, D_MODEL)

    cos_p, sin_p = _rope_tables(jnp.arange(t_prompt))
    cos_p, sin_p = jnp.tile(cos_p, (n_prompt, 1)), jnp.tile(sin_p, (n_prompt, 1))
    cos_s, sin_s = _rope_tables(past_len + jnp.arange(t_sample))
    cos_s, sin_s = jnp.tile(cos_s, (n_sample, 1)), jnp.tile(sin_s, (n_sample, 1))

    outs = {k: [] for k in ("cmlp_v", "k_p", "v_p", "k_s", "v_s", "conv_p", "ssm_p", "conv_s", "ssm_s")}
    for i in range(DEPTH):
        w_in, w_out = ffn_w_in[i, 0].astype(BF16), ffn_w_out[i, 0].astype(BF16)
        hp = _ffn(hp, w_in, w_out, _row(ln_g[i, 0]), _row(ln_b[i, 0]))
        hs = _ffn(hs, w_in, w_out, _row(ln_g[i, 0]), _row(ln_b[i, 0]))
        kind, j = i % 3, i // 3
        g1, b1 = _row(ln_g[i, 1]), _row(ln_b[i, 1])
        if kind == 0:
            samples_per_chunk = CMLP_CHUNK // t_sample
            w_sp_s = jnp.einsum("ab,gts->gatbs", jnp.eye(samples_per_chunk, dtype=F32),
                                cmlp_w_s[j][:, :t_sample, :t_sample]).reshape(CMLP_GROUPS, CMLP_CHUNK, CMLP_CHUNK)
            bias_p = jnp.repeat(cmlp_b_s[j].T, CMLP_DV // CMLP_GROUPS, axis=1)
            bias_s = jnp.tile(jnp.repeat(cmlp_b_s[j][:, :t_sample].T, CMLP_DV // CMLP_GROUPS, axis=1),
                              (samples_per_chunk, 1))
            a_par = (cmlp_w_in[j].astype(BF16), _row(cmlp_ln_g[j]), _row(cmlp_ln_b[j]))
            w_o = cmlp_w_out[j].astype(BF16)
            hp, _ = _cmlp(hp, *a_par, cmlp_w_s[j], bias_p, w_o, g1, b1)
            hs, v_rows = _cmlp(hs, *a_par, w_sp_s, bias_s, w_o, g1, b1)
            outs["cmlp_v"].append(v_rows.reshape(n_sample, t_sample, CMLP_DV))
        elif kind == 1:
            w_qkv = moba_w_qkv[j].astype(BF16)
            w_o = moba_w_out[j].astype(BF16)
            q, k, v, kb, vb, kmean = _qkv(hp, w_qkv, cos_p, sin_p)
            shp = (n_prompt, t_prompt, D_MODEL)
            o = _attn_prompt(q.reshape(shp), kb.reshape(shp), vb.reshape(shp),
                             kmean.reshape(n_prompt, t_prompt // MOBA_BLOCK, D_MODEL))
            hp = _proj_ln(hp, o.reshape(-1, D_MODEL), w_o, g1, b1)
            outs["k_p"].append(k.reshape(n_prompt, t_prompt, HEADS, HEAD_DIM))
            outs["v_p"].append(v.reshape(n_prompt, t_prompt, HEADS, HEAD_DIM))
            q, k, v, _, _, _ = _qkv(hs, w_qkv, cos_s, sin_s)
            shs = (n_sample, t_sample, D_MODEL)
            pool = cache_k.shape[1]
            o = _attn_sample(q.reshape(shs), k.reshape(shs), v.reshape(shs),
                             cache_k[j].reshape(pool, PAGE_SIZE, D_MODEL),
                             cache_v[j].reshape(pool, PAGE_SIZE, D_MODEL), page_table)
            hs = _proj_ln(hs, o.reshape(-1, D_MODEL), w_o, g1, b1)
            outs["k_s"].append(k.reshape(n_sample, t_sample, HEADS, HEAD_DIM))
            outs["v_s"].append(v.reshape(n_sample, t_sample, HEADS, HEAD_DIM))
        else:
            w = ssm_w_in[j]
            w_z = w[:, :SSM_D_INNER].astype(BF16)
            w_x = w[:, SSM_D_INNER:SSM_D_INNER + SSM_CONV_DIM].astype(BF16)
            w_dt = jnp.pad(w[:, SSM_D_INNER + SSM_CONV_DIM:], ((0, 0), (0, LANES - SSM_HEADS))).astype(BF16)
            c_par = (
                jnp.pad(ssm_w_conv[j], ((0, SUBLANES - SSM_CONV), (0, 0))),
                _row(ssm_b_conv[j]),
                _row(jnp.pad(ssm_dt_bias[j], (0, LANES - SSM_HEADS))),
                _row(jnp.pad(ssm_a_log[j], (0, LANES - SSM_HEADS))),
                _row(jnp.repeat(ssm_d[j], SSM_HEAD_DIM)),
                _row(ssm_norm_g[j]),
            )
            w_o = ssm_w_out[j].astype(BF16)
            halo_pad = ((0, 0), (SUBLANES - (SSM_CONV - 1), 0), (0, 0))

            z, raw, dt_raw = _ssm_in(hp, w_z, w_x, w_dt)
            y, conv_t, ssm_f = _ssd(
                raw.reshape(n_prompt, t_prompt, -1), z.reshape(n_prompt, t_prompt, -1),
                dt_raw.reshape(n_prompt, t_prompt, -1),
                jnp.zeros((n_prompt, SUBLANES, SSM_CONV_DIM), F32),
                jnp.zeros((n_prompt, SSM_D_INNER, SSM_D_STATE), F32), *c_par)
            hp = _proj_ln(hp, y.reshape(-1, SSM_D_INNER), w_o, g1, b1)
            outs["conv_p"].append(conv_t[:, SUBLANES - (SSM_CONV - 1):])
            outs["ssm_p"].append(ssm_f.reshape(n_prompt, SSM_HEADS, SSM_HEAD_DIM, SSM_D_STATE))

            z, raw, dt_raw = _ssm_in(hs, w_z, w_x, w_dt)
            y, conv_t, ssm_f = _ssd(
                raw.reshape(n_sample, t_sample, -1), z.reshape(n_sample, t_sample, -1),
                dt_raw.reshape(n_sample, t_sample, -1),
                jnp.pad(state_conv[j], halo_pad),
                state_ssm[j].reshape(n_sample, SSM_D_INNER, SSM_D_STATE), *c_par)
            hs = _proj_ln(hs, y.reshape(-1, SSM_D_INNER), w_o, g1, b1)
            outs["conv_s"].append(conv_t[:, SUBLANES - (SSM_CONV - 1):])
            outs["ssm_s"].append(ssm_f.reshape(n_sample, SSM_HEADS, SSM_HEAD_DIM, SSM_D_STATE))
        w_in, w_out = ffn_w_in[i, 1].astype(BF16), ffn_w_out[i, 1].astype(BF16)
        hp = _ffn(hp, w_in, w_out, _row(ln_g[i, 2]), _row(ln_b[i, 2]))
        hs = _ffn(hs, w_in, w_out, _row(ln_g[i, 2]), _row(ln_b[i, 2]))
    st = lambda name: jnp.stack(outs[name])
    return (hp.reshape(n_prompt, t_prompt, D_MODEL), hs.reshape(n_sample, t_sample, D_MODEL),
            st("cmlp_v"), st("k_p"), st("v_p"), st("k_s"), st("v_s"),
            st("conv_p"), st("ssm_p"), st("conv_s"), st("ssm_s"))
```

```python
import functools
import math

import jax
import jax.numpy as jnp
from jax import lax
from jax.experimental import pallas as pl
from jax.experimental.pallas import tpu as pltpu

F32 = jnp.float32
BF16 = jnp.bfloat16
HIGHEST = lax.Precision.HIGHEST

LANES = 128
SUBLANES = 8
VMEM_LIMIT_BYTES = 56 * 1024 * 1024

DEPTH = 4
D_MODEL = 1024
D_FF = 2816
DEEPNORM_ALPHA = (2 * DEPTH) ** 0.25
LN_EPS = 1e-5
RMS_EPS = 1e-5
FFN_HALF = 0.5

CMLP_CHUNK = 128
CMLP_GROUPS = 8
CMLP_DV = D_MODEL

HEAD_DIM = 128
HEADS = D_MODEL // HEAD_DIM
MOBA_BLOCK = 256
MOBA_TOPK = 3
PAGE_SIZE = 128
ROPE_THETA = 10000.0

SSM_D_INNER = 2 * D_MODEL
SSM_HEAD_DIM = 64
SSM_HEADS = SSM_D_INNER // SSM_HEAD_DIM
SSM_GROUPS = 8
SSM_HPG = SSM_HEADS // SSM_GROUPS
SSM_D_STATE = 128
SSM_CONV = 4
SSM_BC_DIM = SSM_GROUPS * SSM_D_STATE
SSM_CONV_DIM = SSM_D_INNER + 2 * SSM_BC_DIM
SSM_CHUNK = 128

NEG_INF = float("-inf")


def _params(semantics):
    return pltpu.CompilerParams(dimension_semantics=semantics, vmem_limit_bytes=VMEM_LIMIT_BYTES)


def _resident(shape):
    zeros = (0,) * len(shape)
    return pl.BlockSpec(shape, lambda *_: zeros, pipeline_mode=pl.Buffered(1))


def _layer_norm(y, g, b):
    mu = jnp.mean(y, axis=-1, keepdims=True)
    d = y - mu
    var = jnp.mean(d * d, axis=-1, keepdims=True)
    return d * lax.rsqrt(var + LN_EPS) * g + b


def _silu(x):
    return x * jax.nn.sigmoid(x)


FFN_COL_CHUNK = 256


def _ffn_kernel(x_ref, win_ref, wout_ref, g_ref, b_ref, o_ref, act_ref):
    x = x_ref[...]
    xb = x.astype(BF16)
    for c in range(D_FF // FFN_COL_CHUNK):
        lo = c * FFN_COL_CHUNK
        gate = jnp.dot(xb, win_ref[:, lo:lo + FFN_COL_CHUNK], preferred_element_type=F32)
        up = jnp.dot(xb, win_ref[:, D_FF + lo:D_FF + lo + FFN_COL_CHUNK], preferred_element_type=F32)
        act_ref[:, lo:lo + FFN_COL_CHUNK] = (_silu(gate) * up).astype(BF16)
    y = jnp.dot(act_ref[...], wout_ref[...], preferred_element_type=F32)
    o_ref[...] = _layer_norm(DEEPNORM_ALPHA * x + FFN_HALF * y, g_ref[...], b_ref[...])


def _ffn(h, w_in, w_out, g, b, *, tm=512):
    n = h.shape[0]
    return pl.pallas_call(
        _ffn_kernel,
        grid=(n // tm,),
        in_specs=[
            pl.BlockSpec((tm, D_MODEL), lambda i: (i, 0)),
            _resident((D_MODEL, 2 * D_FF)),
            _resident((D_FF, D_MODEL)),
            _resident((1, D_MODEL)),
            _resident((1, D_MODEL)),
        ],
        out_specs=pl.BlockSpec((tm, D_MODEL), lambda i: (i, 0)),
        out_shape=jax.ShapeDtypeStruct((n, D_MODEL), F32),
        scratch_shapes=[pltpu.VMEM((tm, D_FF), BF16)],
        compiler_params=_params(("parallel",)),
        name="ffn",
    )(h, w_in, w_out, g, b)


def _proj_ln_kernel(h_ref, pre_ref, w_ref, g_ref, b_ref, o_ref, *, pre_transposed):
    contract_pre = 0 if pre_transposed else 1
    y = lax.dot_general(pre_ref[...], w_ref[...], (((contract_pre,), (0,)), ((), ())),
                        preferred_element_type=F32)
    o_ref[...] = _layer_norm(DEEPNORM_ALPHA * h_ref[...] + y, g_ref[...], b_ref[...])


def _proj_ln(h, pre, w_out, g, b, *, pre_transposed=False, tm=512):
    n = h.shape[0]
    k = w_out.shape[0]
    pre_spec = (pl.BlockSpec((k, tm), lambda i: (0, i)) if pre_transposed
                else pl.BlockSpec((tm, k), lambda i: (i, 0)))
    return pl.pallas_call(
        functools.partial(_proj_ln_kernel, pre_transposed=pre_transposed),
        grid=(n // tm,),
        in_specs=[
            pl.BlockSpec((tm, D_MODEL), lambda i: (i, 0)),
            pre_spec,
            _resident((k, D_MODEL)),
            _resident((1, D_MODEL)),
            _resident((1, D_MODEL)),
        ],
        out_specs=pl.BlockSpec((tm, D_MODEL), lambda i: (i, 0)),
        out_shape=jax.ShapeDtypeStruct((n, D_MODEL), F32),
        compiler_params=_params(("parallel",)),
        name="proj_ln",
    )(h, pre, w_out, g, b)


def _cmlp_kernel(h_ref, win_ref, lng_ref, lnb_ref, wsp_ref, bias_ref, wout_ref, g_ref, b_ref,
                 o_ref, *rest):
    *v_refs, pre_ref = rest
    tm = h_ref.shape[0]
    x = h_ref[...]
    uv = jnp.dot(x.astype(BF16), win_ref[...], preferred_element_type=F32)
    uv = 0.5 * uv * (1.0 + lax.erf(uv * math.sqrt(0.5)))
    u = uv[:, :CMLP_DV]
    v = _layer_norm(uv[:, CMLP_DV:], lng_ref[...], lnb_ref[...])
    for v_ref in v_refs:
        v_ref[...] = v
    vb = v.astype(BF16)
    row = lax.broadcasted_iota(jnp.int32, (CMLP_CHUNK, CMLP_CHUNK), 0)
    col = lax.broadcasted_iota(jnp.int32, (CMLP_CHUNK, CMLP_CHUNK), 1)
    causal = col <= row
    for g in range(CMLP_GROUPS):
        gl = g * LANES
        wg = jnp.where(causal, wsp_ref[g], 0.0).astype(BF16)
        for c in range(tm // CMLP_CHUNK):
            cl = c * CMLP_CHUNK
            s = jnp.dot(wg, vb[cl:cl + CMLP_CHUNK, gl:gl + LANES], preferred_element_type=F32)
            s = s + bias_ref[:, gl:gl + LANES]
            pre_ref[cl:cl + CMLP_CHUNK, gl:gl + LANES] = (u[cl:cl + CMLP_CHUNK, gl:gl + LANES] * s).astype(BF16)
    y = jnp.dot(pre_ref[...], wout_ref[...], preferred_element_type=F32)
    o_ref[...] = _layer_norm(DEEPNORM_ALPHA * x + y, g_ref[...], b_ref[...])


def _cmlp(h, w_in, ln_g, ln_b, w_sp, bias_full, w_out, g, b, *, emit_v, tm=256):
    n = h.shape[0]
    n_out = 2 if emit_v else 1
    return pl.pallas_call(
        _cmlp_kernel,
        grid=(n // tm,),
        in_specs=[
            pl.BlockSpec((tm, D_MODEL), lambda i: (i, 0)),
            _resident((D_MODEL, 2 * CMLP_DV)),
            _resident((1, CMLP_DV)),
            _resident((1, CMLP_DV)),
            _resident((CMLP_GROUPS, CMLP_CHUNK, CMLP_CHUNK)),
            _resident((CMLP_CHUNK, CMLP_DV)),
            _resident((CMLP_DV, D_MODEL)),
            _resident((1, D_MODEL)),
            _resident((1, D_MODEL)),
        ],
        out_specs=[pl.BlockSpec((tm, D_MODEL), lambda i: (i, 0))] * n_out,
        out_shape=[jax.ShapeDtypeStruct((n, D_MODEL), F32)] * n_out,
        scratch_shapes=[pltpu.VMEM((tm, CMLP_DV), BF16)],
        compiler_params=_params(("parallel",)),
        name="cmlp",
    )(h, w_in, ln_g, ln_b, w_sp, bias_full, w_out, g, b)


def _qkv_kernel(h_ref, w_ref, cos_ref, sin_ref, q_ref, k_ref, v_ref, *attn_refs):
    tm = h_ref.shape[0]
    qkv = jnp.dot(h_ref[...].astype(BF16), w_ref[...], preferred_element_type=F32)
    cos = cos_ref[...]
    sin = sin_ref[...]
    for hd in range(HEADS):
        lo = hd * HEAD_DIM
        qh = qkv[:, lo:lo + HEAD_DIM]
        q_ref[:, lo:lo + HEAD_DIM] = (qh * cos + pltpu.roll(qh, HEAD_DIM // 2, axis=1) * sin).astype(BF16)
        kh = qkv[:, D_MODEL + lo:D_MODEL + lo + HEAD_DIM]
        k_ref[:, lo:lo + HEAD_DIM] = kh * cos + pltpu.roll(kh, HEAD_DIM // 2, axis=1) * sin
    v = qkv[:, 2 * D_MODEL:]
    v_ref[...] = v
    if attn_refs:
        kb_ref, vt_ref, km_ref = attn_refs
        kb_ref[...] = k_ref[...].astype(BF16)
        vt_ref[...] = v.T.astype(BF16)
        for j in range(tm // MOBA_BLOCK):
            blk = k_ref[j * MOBA_BLOCK:(j + 1) * MOBA_BLOCK, :]
            km_ref[0, j:j + 1, :] = jnp.sum(blk, axis=0, keepdims=True) * (1.0 / MOBA_BLOCK)


def _qkv(h, w_qkv, cos, sin, *, for_prompt, tm=512):
    n = h.shape[0]
    nkb = tm // MOBA_BLOCK
    row_spec = lambda width: pl.BlockSpec((tm, width), lambda i: (i, 0))
    out_specs = [row_spec(D_MODEL)] * 3
    out_shape = [
        jax.ShapeDtypeStruct((n, D_MODEL), BF16),
        jax.ShapeDtypeStruct((n, D_MODEL), F32),
        jax.ShapeDtypeStruct((n, D_MODEL), F32),
    ]
    if for_prompt:
        out_specs += [
            row_spec(D_MODEL),
            pl.BlockSpec((D_MODEL, tm), lambda i: (0, i)),
            pl.BlockSpec((1, nkb, D_MODEL), lambda i: (i, 0, 0)),
        ]
        out_shape += [
            jax.ShapeDtypeStruct((n, D_MODEL), BF16),
            jax.ShapeDtypeStruct((D_MODEL, n), BF16),
            jax.ShapeDtypeStruct((n // tm, nkb, D_MODEL), F32),
        ]
    return pl.pallas_call(
        _qkv_kernel,
        grid=(n // tm,),
        in_specs=[
            row_spec(D_MODEL),
            _resident((D_MODEL, 3 * D_MODEL)),
            row_spec(HEAD_DIM),
            row_spec(HEAD_DIM),
        ],
        out_specs=out_specs,
        out_shape=out_shape,
        compiler_params=_params(("parallel",)),
        name="qkv_rope",
    )(h, w_qkv, cos, sin)


def _top3_sublanes(gate, n_valid):
    n_rows = gate.shape[0]
    blk = lax.broadcasted_iota(jnp.int32, gate.shape, 0)
    gate = jnp.where(blk < n_valid, gate, NEG_INF)
    blk = blk.astype(F32)
    bias = jnp.full(gate.shape, NEG_INF, F32)
    for _ in range(MOBA_TOPK):
        m = jnp.max(gate, axis=0, keepdims=True)
        first = jnp.min(jnp.where(gate == m, blk, float(n_rows)), axis=0, keepdims=True)
        pick = blk == first
        bias = jnp.where(pick & (m > NEG_INF), 0.0, bias)
        gate = jnp.where(pick, NEG_INF, gate)
    return bias


def _top3_lanes(gate, n_valid):
    lane = lax.broadcasted_iota(jnp.int32, gate.shape, 1)
    gate = jnp.where(lane < n_valid, gate, NEG_INF)
    bias = jnp.full(gate.shape, NEG_INF, F32)
    for _ in range(MOBA_TOPK):
        m = jnp.max(gate, axis=1, keepdims=True)
        first = jnp.min(jnp.where(gate == m, lane, LANES), axis=1, keepdims=True)
        pick = lane == first
        bias = jnp.where(pick & (m > NEG_INF), 0.0, bias)
        gate = jnp.where(pick, NEG_INF, gate)
    return bias


def _attn_prompt_kernel(q_ref, k_ref, vt_ref, km_ref, o_ref, bias_ref, acc_ref):
    i = pl.program_id(1)
    tq = q_ref.shape[0]
    c_exp = HEAD_DIM ** -0.5 * math.log2(math.e)
    nt = (((1,), (1,)), ((), ()))

    def scores(h, start):
        lo = h * HEAD_DIM
        kj = k_ref[0, pl.ds(start, MOBA_BLOCK), lo:lo + HEAD_DIM]
        return lax.dot_general(kj, q_ref[:, lo:lo + HEAD_DIM], nt, preferred_element_type=F32)

    def weighted_values(h, start, p):
        lo = h * HEAD_DIM
        return jnp.dot(vt_ref[lo:lo + HEAD_DIM, pl.ds(start, MOBA_BLOCK)], p.astype(BF16),
                       preferred_element_type=F32)

    own = pl.multiple_of(i * MOBA_BLOCK, MOBA_BLOCK)
    key = lax.broadcasted_iota(jnp.int32, (MOBA_BLOCK, tq), 0)
    qry = lax.broadcasted_iota(jnp.int32, (MOBA_BLOCK, tq), 1)
    ms, ls = [], []
    for h in range(HEADS):
        lo = h * HEAD_DIM
        gate = lax.dot_general(km_ref[0, :, lo:lo + HEAD_DIM], q_ref[:, lo:lo + HEAD_DIM].astype(F32), nt,
                               precision=HIGHEST, preferred_element_type=F32)
        bias_ref[h] = _top3_sublanes(gate, i)
        s = jnp.where(key <= qry, scores(h, own), NEG_INF)
        m = jnp.max(s, axis=0, keepdims=True)
        p = jnp.exp2((s - m) * c_exp)
        ms.append(m)
        ls.append(jnp.sum(p, axis=0, keepdims=True))
        acc_ref[h] = weighted_values(h, own, p)

    def body(j, carry):
        ms, ls = carry
        start = pl.multiple_of(j * MOBA_BLOCK, MOBA_BLOCK)
        new_ms, new_ls, alphas, ps = [], [], [], []
        ss = [scores(h, start) for h in range(HEADS)]
        for h in range(HEADS):
            s = ss[h] + bias_ref[h, pl.ds(j, 1), :]
            m_new = jnp.maximum(ms[h], jnp.max(s, axis=0, keepdims=True))
            alpha = jnp.exp2((ms[h] - m_new) * c_exp)
            p = jnp.exp2((s - m_new) * c_exp)
            new_ms.append(m_new)
            new_ls.append(alpha * ls[h] + jnp.sum(p, axis=0, keepdims=True))
            alphas.append(alpha)
            ps.append(p.astype(BF16))
        for h in range(HEADS):
            acc_ref[h] = alphas[h] * acc_ref[h] + weighted_values(h, start, ps[h])
        return tuple(new_ms), tuple(new_ls)

    _, ls = lax.fori_loop(0, i, body, (tuple(ms), tuple(ls)))
    for h in range(HEADS):
        o_ref[h * HEAD_DIM:(h + 1) * HEAD_DIM, :] = (acc_ref[h] / ls[h]).astype(BF16)


def _attn_prompt(q, k, vt, kmean):
    bn, t, _ = k.shape
    nb = t // MOBA_BLOCK
    tq = MOBA_BLOCK
    nq = t // tq
    return pl.pallas_call(
        _attn_prompt_kernel,
        grid=(bn, nq),
        in_specs=[
            pl.BlockSpec((tq, D_MODEL), lambda b, i: (b * nq + i, 0)),
            pl.BlockSpec((1, t, D_MODEL), lambda b, i: (b, 0, 0), pipeline_mode=pl.Buffered(1)),
            pl.BlockSpec((D_MODEL, t), lambda b, i: (0, b), pipeline_mode=pl.Buffered(1)),
            pl.BlockSpec((1, nb, D_MODEL), lambda b, i: (b, 0, 0)),
        ],
        out_specs=pl.BlockSpec((D_MODEL, tq), lambda b, i: (0, b * nq + i)),
        out_shape=jax.ShapeDtypeStruct((D_MODEL, bn * t), BF16),
        scratch_shapes=[
            pltpu.VMEM((HEADS, nb, tq), F32),
            pltpu.VMEM((HEADS, HEAD_DIM, tq), F32),
        ],
        compiler_params=_params(("parallel", "arbitrary")),
        name="moba_prompt",
    )(q, k, vt, kmean)


def _attn_sample_kernel(pt_ref, q_ref, kn_ref, vn_ref, *rest, n_pages):
    kp_refs = rest[:n_pages]
    vp_refs = rest[n_pages:2 * n_pages]
    o_ref, kb_ref, vb_ref, s_ref = rest[2 * n_pages:]
    del pt_ref
    tq = q_ref.shape[1]
    past = n_pages * PAGE_SIZE
    n_past_blocks = past // MOBA_BLOCK
    pages_per_block = MOBA_BLOCK // PAGE_SIZE
    n_new = 2 * SUBLANES
    scale = HEAD_DIM ** -0.5

    q = q_ref[0].astype(F32)
    qt = jnp.concatenate([q] * (LANES // tq), axis=0)
    rid = lax.broadcasted_iota(jnp.int32, (LANES, D_MODEL), 0)
    lid = lax.broadcasted_iota(jnp.int32, (LANES, D_MODEL), 1)
    qr = jnp.where((rid < HEADS * tq) & (lid // HEAD_DIM == rid // tq), qt, 0.0).astype(BF16)

    kmean_rows = []
    for n in range(n_past_blocks):
        acc = jnp.zeros((1, D_MODEL), F32)
        for r in range(pages_per_block):
            p = n * pages_per_block + r
            kp = kp_refs[p][0]
            kb_ref[p * PAGE_SIZE:(p + 1) * PAGE_SIZE, :] = kp.astype(BF16)
            vb_ref[p * PAGE_SIZE:(p + 1) * PAGE_SIZE, :] = vp_refs[p][0].astype(BF16)
            acc = acc + jnp.sum(kp, axis=0, keepdims=True)
        kmean_rows.append(acc * (1.0 / MOBA_BLOCK))
    kmean = jnp.concatenate(kmean_rows + [jnp.zeros((LANES - n_past_blocks, D_MODEL), F32)], axis=0)
    zpad = jnp.zeros((n_new - tq, D_MODEL), F32)
    kb_ref[past:past + n_new, :] = jnp.concatenate([kn_ref[0], zpad], axis=0).astype(BF16)
    vb_ref[past:past + n_new, :] = jnp.concatenate([vn_ref[0], zpad], axis=0).astype(BF16)

    nt = (((1,), (1,)), ((), ()))
    gate = lax.dot_general(qr.astype(F32), kmean, nt, precision=HIGHEST, preferred_element_type=F32)
    bias = _top3_lanes(gate, n_past_blocks)

    s_ref[:, :past] = lax.dot_general(qr, kb_ref[:past, :], nt, preferred_element_type=F32) * scale
    s_new = lax.dot_general(qr, kb_ref[past:past + n_new, :], nt, preferred_element_type=F32) * scale
    qi = lax.broadcasted_iota(jnp.int32, (LANES, n_new), 0) % tq
    kr = lax.broadcasted_iota(jnp.int32, (LANES, n_new), 1)
    s_new = jnp.where(kr <= qi, s_new, NEG_INF)
    m = jnp.max(s_new, axis=1, keepdims=True)
    for n in range(n_past_blocks):
        lo = n * MOBA_BLOCK
        sb = s_ref[:, lo:lo + MOBA_BLOCK] + bias[:, n:n + 1]
        s_ref[:, lo:lo + MOBA_BLOCK] = sb
        m = jnp.maximum(m, jnp.max(sb, axis=1, keepdims=True))
    p_new = jnp.exp(s_new - m)
    l = jnp.sum(p_new, axis=1, keepdims=True)
    for n in range(n_past_blocks):
        lo = n * MOBA_BLOCK
        pb = jnp.exp(s_ref[:, lo:lo + MOBA_BLOCK] - m)
        s_ref[:, lo:lo + MOBA_BLOCK] = pb
        l = l + jnp.sum(pb, axis=1, keepdims=True)
    inv_l = 1.0 / l
    out = jnp.dot((s_ref[:, :past] * inv_l).astype(BF16), vb_ref[:past, :], preferred_element_type=F32)
    out = out + jnp.dot((p_new * inv_l).astype(BF16), vb_ref[past:past + n_new, :], preferred_element_type=F32)
    for hd in range(HEADS):
        lo = hd * HEAD_DIM
        o_ref[0, :, lo:lo + HEAD_DIM] = out[hd * tq:(hd + 1) * tq, lo:lo + HEAD_DIM].astype(BF16)


def _attn_sample(q, k_new, v_new, cache_k, cache_v, page_table):
    ns, tq, _ = q.shape
    n_pages = page_table.shape[1]
    past = n_pages * PAGE_SIZE
    n_new = 2 * SUBLANES
    assert tq == SUBLANES and HEADS * tq <= LANES and past % MOBA_BLOCK == 0

    def page_spec(p):
        return pl.BlockSpec((1, PAGE_SIZE, D_MODEL), lambda s, pt: (pt[s * n_pages + p], 0, 0))

    seq_spec = pl.BlockSpec((1, tq, D_MODEL), lambda s, pt: (s, 0, 0))
    grid_spec = pltpu.PrefetchScalarGridSpec(
        num_scalar_prefetch=1,
        grid=(ns,),
        in_specs=[seq_spec, seq_spec, seq_spec]
        + [page_spec(p) for p in range(n_pages)]
        + [page_spec(p) for p in range(n_pages)],
        out_specs=seq_spec,
        scratch_shapes=[
            pltpu.VMEM((past + n_new, D_MODEL), BF16),
            pltpu.VMEM((past + n_new, D_MODEL), BF16),
            pltpu.VMEM((LANES, past), F32),
        ],
    )
    return pl.pallas_call(
        functools.partial(_attn_sample_kernel, n_pages=n_pages),
        grid_spec=grid_spec,
        out_shape=jax.ShapeDtypeStruct((ns, tq, D_MODEL), BF16),
        compiler_params=_params(("arbitrary",)),
        name="moba_sample",
    )(page_table.reshape(-1), q, k_new, v_new, *([cache_k] * n_pages), *([cache_v] * n_pages))


def _ssm_in_kernel(h_ref, wz_ref, wx_ref, wdt_ref, z_ref, raw_ref, dt_ref):
    xb = h_ref[...].astype(BF16)
    z_ref[...] = jnp.dot(xb, wz_ref[...], preferred_element_type=F32)
    raw_ref[...] = jnp.dot(xb, wx_ref[...], preferred_element_type=F32)
    dt_ref[...] = jnp.dot(xb, wdt_ref[...], preferred_element_type=F32)


def _ssm_in(h, w_z, w_x, w_dt, *, tm=256):
    n = h.shape[0]
    row_spec = lambda width: pl.BlockSpec((tm, width), lambda i: (i, 0))
    return pl.pallas_call(
        _ssm_in_kernel,
        grid=(n // tm,),
        in_specs=[
            row_spec(D_MODEL),
            _resident((D_MODEL, SSM_D_INNER)),
            _resident((D_MODEL, SSM_CONV_DIM)),
            _resident((D_MODEL, LANES)),
        ],
        out_specs=[row_spec(SSM_D_INNER), row_spec(SSM_CONV_DIM), row_spec(LANES)],
        out_shape=[
            jax.ShapeDtypeStruct((n, SSM_D_INNER), F32),
            jax.ShapeDtypeStruct((n, SSM_CONV_DIM), F32),
            jax.ShapeDtypeStruct((n, LANES), F32),
        ],
        compiler_params=_params(("parallel",)),
        name="ssm_in",
    )(h, w_z, w_x, w_dt)


def _ssd_kernel(raw_ref, z_ref, dtr_ref, conv0_ref, ssm0_ref, wconv_ref, bconv_ref, dtb_ref, alog_ref,
                dskip_ref, ng_ref, tri_ref,
                y_ref, convo_ref, ssmo_ref, ext_ref, state_ref, yacc_ref):
    c = pl.program_id(1)
    t = raw_ref.shape[0]
    L = SSM_CHUNK
    halo = SUBLANES

    @pl.when(c == 0)
    def _():
        ext_ref[0:halo, :] = conv0_ref[0]
        state_ref[...] = ssm0_ref[0]

    ext_ref[halo:halo + t, :] = raw_ref[...]
    if t < L:
        ext_ref[halo + t:halo + L, :] = jnp.zeros((L - t, SSM_CONV_DIM), F32)
    conv = bconv_ref[...]
    for w in range(SSM_CONV):
        lo = halo - (SSM_CONV - 1) + w
        conv = conv + wconv_ref[w:w + 1, :] * ext_ref[lo:lo + L, :]
    tail = ext_ref[t:t + halo, :]
    ext_ref[0:halo, :] = tail
    convo_ref[0] = tail
    xbc = _silu(conv)
    x = xbc[:, :SSM_D_INNER]
    bm = xbc[:, SSM_D_INNER:SSM_D_INNER + SSM_BC_DIM].astype(BF16)
    cm = xbc[:, SSM_D_INNER + SSM_BC_DIM:]

    dtx = dtr_ref[...] + dtb_ref[...]
    dt = jnp.maximum(dtx, 0.0) + jnp.log1p(jnp.exp(-jnp.abs(dtx)))
    if t < L:
        dt = jnp.concatenate([dt, jnp.zeros((L - t, LANES), F32)], axis=0)
    a = -jnp.exp(alog_ref[...])
    adt = dt * a
    acs = jnp.dot(tri_ref[...], adt, precision=HIGHEST, preferred_element_type=F32)
    acs_t = acs.T
    dt_t = dt.T
    w_t = jnp.exp(acs_t[:, L - 1:L] - acs_t) * dt_t
    x_t = x.T
    xb = x.astype(BF16)

    row = lax.broadcasted_iota(jnp.int32, (L, L), 0)
    col = lax.broadcasted_iota(jnp.int32, (L, L), 1)
    causal = col <= row
    nt = (((1,), (1,)), ((), ()))
    for g in range(SSM_GROUPS):
        bg = bm[:, g * SSM_D_STATE:(g + 1) * SSM_D_STATE]
        cg = cm[:, g * SSM_D_STATE:(g + 1) * SSM_D_STATE]
        cb = lax.dot_general(cg.astype(BF16), bg, nt, preferred_element_type=F32)
        for e in range(SSM_HPG):
            hd = g * SSM_HPG + e
            lo = hd * SSM_HEAD_DIM
            acs_col = jnp.broadcast_to(acs[:, hd:hd + 1], (L, L))
            lmat = jnp.where(causal, jnp.exp(acs_col - acs_t[hd:hd + 1, :]), 0.0)
            mh = (cb * lmat * dt_t[hd:hd + 1, :]).astype(BF16)
            ch = (cg * jnp.exp(acs_col)).astype(BF16)
            sh = state_ref[lo:lo + SSM_HEAD_DIM, :]
            yh = jnp.dot(mh, xb[:, lo:lo + SSM_HEAD_DIM], preferred_element_type=F32)
            yh = yh + lax.dot_general(ch, sh.astype(BF16), nt, preferred_element_type=F32)
            yacc_ref[:, lo:lo + SSM_HEAD_DIM] = yh
            xw = (x_t[lo:lo + SSM_HEAD_DIM, :] * w_t[hd:hd + 1, :]).astype(BF16)
            upd = jnp.dot(xw, bg, preferred_element_type=F32)
            state_ref[lo:lo + SSM_HEAD_DIM, :] = jnp.exp(acs_t[hd:hd + 1, L - 1:L]) * sh + upd
    ssmo_ref[0] = state_ref[...]

    y = yacc_ref[0:t, :] + x[:t] * dskip_ref[...]
    y = y * _silu(z_ref[...])
    gw = SSM_D_INNER // SSM_GROUPS
    for g in range(SSM_GROUPS):
        yg = y[:, g * gw:(g + 1) * gw]
        yg = yg * lax.rsqrt(jnp.mean(yg * yg, axis=-1, keepdims=True) + RMS_EPS)
        y_ref[:, g * gw:(g + 1) * gw] = (yg * ng_ref[:, g * gw:(g + 1) * gw]).astype(y_ref.dtype)


def _ssd(raw, z, dt_raw, conv0, ssm0, w_conv, b_conv, dt_bias, a_log, d_skip, norm_g):
    bn, t_total, _ = raw.shape
    t = min(t_total, SSM_CHUNK)
    nc = t_total // t
    tri = jnp.tril(jnp.ones((SSM_CHUNK, SSM_CHUNK), F32))
    blk = lambda width: pl.BlockSpec((None, t, width), lambda b, c: (b, c, 0))
    per_seq = lambda rows, width: pl.BlockSpec((1, rows, width), lambda b, c: (b, 0, 0))
    return pl.pallas_call(
        _ssd_kernel,
        grid=(bn, nc),
        in_specs=[
            blk(SSM_CONV_DIM), blk(SSM_D_INNER), blk(LANES),
            per_seq(SUBLANES, SSM_CONV_DIM),
            per_seq(SSM_D_INNER, SSM_D_STATE),
            _resident((SUBLANES, SSM_CONV_DIM)),
            _resident((1, SSM_CONV_DIM)),
            _resident((1, LANES)),
            _resident((1, LANES)),
            _resident((1, SSM_D_INNER)),
            _resident((1, SSM_D_INNER)),
            _resident((SSM_CHUNK, SSM_CHUNK)),
        ],
        out_specs=[
            blk(SSM_D_INNER),
            per_seq(SUBLANES, SSM_CONV_DIM),
            per_seq(SSM_D_INNER, SSM_D_STATE),
        ],
        out_shape=[
            jax.ShapeDtypeStruct((bn, t_total, SSM_D_INNER), BF16),
            jax.ShapeDtypeStruct((bn, SUBLANES, SSM_CONV_DIM), F32),
            jax.ShapeDtypeStruct((bn, SSM_D_INNER, SSM_D_STATE), F32),
        ],
        scratch_shapes=[
            pltpu.VMEM((SUBLANES + SSM_CHUNK, SSM_CONV_DIM), F32),
            pltpu.VMEM((SSM_D_INNER, SSM_D_STATE), F32),
            pltpu.VMEM((SSM_CHUNK, SSM_D_INNER), F32),
        ],
        compiler_params=_params(("arbitrary", "arbitrary")),
        name="ssd",
    )(raw, z, dt_raw, conv0, ssm0, w_conv, b_conv, dt_bias, a_log, d_skip, norm_g, tri)


def _rope_tables(pos):
    half = HEAD_DIM // 2
    inv = ROPE_THETA ** (-jnp.arange(half, dtype=F32) / half)
    ang = pos.astype(F32)[:, None] * inv[None, :]
    cos, sin = jnp.cos(ang), jnp.sin(ang)
    return jnp.concatenate([cos, cos], axis=-1), jnp.concatenate([-sin, sin], axis=-1)


def _row(v):
    return v.reshape(1, -1)


def kernel(x_prompt, x_sample, cache_k, cache_v, state_conv, state_ssm, page_table, ln_g, ln_b, ffn_w_in, ffn_w_out, cmlp_w_in, cmlp_ln_g, cmlp_ln_b, cmlp_w_s, cmlp_b_s, cmlp_w_out, moba_w_qkv, moba_w_out, ssm_w_in, ssm_w_conv, ssm_b_conv, ssm_dt_bias, ssm_a_log, ssm_d, ssm_norm_g, ssm_w_out):
    n_prompt, t_prompt, _ = x_prompt.shape
    n_sample, t_sample, _ = x_sample.shape
    n_pages = page_table.shape[1]
    past_len = n_pages * PAGE_SIZE
    hp = x_prompt.reshape(n_prompt * t_prompt, D_MODEL)
    hs = x_sample.reshape(n_sample * t_sample, D_MODEL)

    cos_p, sin_p = _rope_tables(jnp.arange(t_prompt))
    cos_p, sin_p = jnp.tile(cos_p, (n_prompt, 1)), jnp.tile(sin_p, (n_prompt, 1))
    cos_s, sin_s = _rope_tables(past_len + jnp.arange(t_sample))
    cos_s, sin_s = jnp.tile(cos_s, (n_sample, 1)), jnp.tile(sin_s, (n_sample, 1))

    outs = {k: [] for k in ("cmlp_v", "k_p", "v_p", "k_s", "v_s", "conv_p", "ssm_p", "conv_s", "ssm_s")}
    for i in range(DEPTH):
        w_in, w_out = ffn_w_in[i, 0].astype(BF16), ffn_w_out[i, 0].astype(BF16)
        hp = _ffn(hp, w_in, w_out, _row(ln_g[i, 0]), _row(ln_b[i, 0]))
        hs = _ffn(hs, w_in, w_out, _row(ln_g[i, 0]), _row(ln_b[i, 0]))
        kind, j = i % 3, i // 3
        g1, b1 = _row(ln_g[i, 1]), _row(ln_b[i, 1])
        if kind == 0:
            samples_per_chunk = CMLP_CHUNK // t_sample
            w_sp_s = jnp.einsum("ab,gts->gatbs", jnp.eye(samples_per_chunk, dtype=F32),
                                cmlp_w_s[j][:, :t_sample, :t_sample]).reshape(CMLP_GROUPS, CMLP_CHUNK, CMLP_CHUNK)
            bias_p = jnp.repeat(cmlp_b_s[j].T, CMLP_DV // CMLP_GROUPS, axis=1)
            bias_s = jnp.tile(jnp.repeat(cmlp_b_s[j][:, :t_sample].T, CMLP_DV // CMLP_GROUPS, axis=1),
                              (samples_per_chunk, 1))
            a_par = (cmlp_w_in[j].astype(BF16), _row(cmlp_ln_g[j]), _row(cmlp_ln_b[j]))
            w_o = cmlp_w_out[j].astype(BF16)
            (hp,) = _cmlp(hp, *a_par, cmlp_w_s[j], bias_p, w_o, g1, b1, emit_v=False)
            hs, v_rows = _cmlp(hs, *a_par, w_sp_s, bias_s, w_o, g1, b1, emit_v=True)
            outs["cmlp_v"].append(v_rows.reshape(n_sample, t_sample, CMLP_DV))
        elif kind == 1:
            w_qkv = moba_w_qkv[j].astype(BF16)
            w_o = moba_w_out[j].astype(BF16)
            q, k, v, kb, vt, kmean = _qkv(hp, w_qkv, cos_p, sin_p, for_prompt=True)
            o_t = _attn_prompt(q, kb.reshape(n_prompt, t_prompt, D_MODEL), vt,
                               kmean.reshape(n_prompt, t_prompt // MOBA_BLOCK, D_MODEL))
            hp = _proj_ln(hp, o_t, w_o, g1, b1, pre_transposed=True)
            outs["k_p"].append(k.reshape(n_prompt, t_prompt, HEADS, HEAD_DIM))
            outs["v_p"].append(v.reshape(n_prompt, t_prompt, HEADS, HEAD_DIM))
            q, k, v = _qkv(hs, w_qkv, cos_s, sin_s, for_prompt=False)
            shs = (n_sample, t_sample, D_MODEL)
            pool = cache_k.shape[1]
            o = _attn_sample(q.reshape(shs), k.reshape(shs), v.reshape(shs),
                             cache_k.reshape(-1, PAGE_SIZE, D_MODEL),
                             cache_v.reshape(-1, PAGE_SIZE, D_MODEL), page_table + j * pool)
            hs = _proj_ln(hs, o.reshape(-1, D_MODEL), w_o, g1, b1)
            outs["k_s"].append(k.reshape(n_sample, t_sample, HEADS, HEAD_DIM))
            outs["v_s"].append(v.reshape(n_sample, t_sample, HEADS, HEAD_DIM))
        else:
            w = ssm_w_in[j]
            w_z = w[:, :SSM_D_INNER].astype(BF16)
            w_x = w[:, SSM_D_INNER:SSM_D_INNER + SSM_CONV_DIM].astype(BF16)
            w_dt = jnp.pad(w[:, SSM_D_INNER + SSM_CONV_DIM:], ((0, 0), (0, LANES - SSM_HEADS))).astype(BF16)
            c_par = (
                jnp.pad(ssm_w_conv[j], ((0, SUBLANES - SSM_CONV), (0, 0))),
                _row(ssm_b_conv[j]),
                _row(jnp.pad(ssm_dt_bias[j], (0, LANES - SSM_HEADS))),
                _row(jnp.pad(ssm_a_log[j], (0, LANES - SSM_HEADS))),
                _row(jnp.repeat(ssm_d[j], SSM_HEAD_DIM)),
                _row(ssm_norm_g[j]),
            )
            w_o = ssm_w_out[j].astype(BF16)
            halo_pad = ((0, 0), (SUBLANES - (SSM_CONV - 1), 0), (0, 0))

            z, raw, dt_raw = _ssm_in(hp, w_z, w_x, w_dt)
            y, conv_t, ssm_f = _ssd(
                raw.reshape(n_prompt, t_prompt, -1), z.reshape(n_prompt, t_prompt, -1),
                dt_raw.reshape(n_prompt, t_prompt, -1),
                jnp.zeros((n_prompt, SUBLANES, SSM_CONV_DIM), F32),
                jnp.zeros((n_prompt, SSM_D_INNER, SSM_D_STATE), F32), *c_par)
            hp = _proj_ln(hp, y.reshape(-1, SSM_D_INNER), w_o, g1, b1)
            outs["conv_p"].append(conv_t[:, SUBLANES - (SSM_CONV - 1):])
            outs["ssm_p"].append(ssm_f.reshape(n_prompt, SSM_HEADS, SSM_HEAD_DIM, SSM_D_STATE))

            z, raw, dt_raw = _ssm_in(hs, w_z, w_x, w_dt)
            y, conv_t, ssm_f = _ssd(
                raw.reshape(n_sample, t_sample, -1), z.reshape(n_sample, t_sample, -1),
                dt_raw.reshape(n_sample, t_sample, -1),
                jnp.pad(state_conv[j], halo_pad),
                state_ssm[j].reshape(n_sample, SSM_D_INNER, SSM_D_STATE), *c_par)
            hs = _proj_ln(hs, y.reshape(-1, SSM_D_INNER), w_o, g1, b1)
            outs["conv_s"].append(conv_t[:, SUBLANES - (SSM_CONV - 1):])
            outs["ssm_s"].append(ssm_f.reshape(n_sample, SSM_HEADS, SSM_HEAD_DIM, SSM_D_STATE))
        w_in, w_out = ffn_w_in[i, 1].astype(BF16), ffn_w_out[i, 1].astype(BF16)
        hp = _ffn(hp, w_in, w_out, _row(ln_g[i, 2]), _row(ln_b[i, 2]))
        hs = _ffn(hs, w_in, w_out, _row(ln_g[i, 2]), _row(ln_b[i, 2]))
    st = lambda name: jnp.stack(outs[name])
    return (hp.reshape(n_prompt, t_prompt, D_MODEL), hs.reshape(n_sample, t_sample, D_MODEL),
            st("cmlp_v"), st("k_p"), st("v_p"), st("k_s"), st("v_s"),
            st("conv_p"), st("ssm_p"), st("conv_s"), st("ssm_s"))
```

```python
import functools
import math

import jax
import jax.numpy as jnp
from jax import lax
from jax.experimental import pallas as pl
from jax.experimental.pallas import tpu as pltpu

F32 = jnp.float32
BF16 = jnp.bfloat16
HIGHEST = lax.Precision.HIGHEST

LANES = 128
SUBLANES = 8
VMEM_LIMIT_BYTES = 56 * 1024 * 1024

DEPTH = 4
D_MODEL = 1024
D_FF = 2816
DEEPNORM_ALPHA = (2 * DEPTH) ** 0.25
LN_EPS = 1e-5
RMS_EPS = 1e-5
FFN_HALF = 0.5

CMLP_CHUNK = 128
CMLP_GROUPS = 8
CMLP_DV = D_MODEL

HEAD_DIM = 128
HEADS = D_MODEL // HEAD_DIM
MOBA_BLOCK = 256
MOBA_TOPK = 3
PAGE_SIZE = 128
ROPE_THETA = 10000.0

SSM_D_INNER = 2 * D_MODEL
SSM_HEAD_DIM = 64
SSM_HEADS = SSM_D_INNER // SSM_HEAD_DIM
SSM_GROUPS = 8
SSM_HPG = SSM_HEADS // SSM_GROUPS
SSM_D_STATE = 128
SSM_CONV = 4
SSM_BC_DIM = SSM_GROUPS * SSM_D_STATE
SSM_CONV_DIM = SSM_D_INNER + 2 * SSM_BC_DIM
SSM_CHUNK = 128

NEG_INF = float("-inf")


def _params(semantics):
    return pltpu.CompilerParams(dimension_semantics=semantics, vmem_limit_bytes=VMEM_LIMIT_BYTES)


def _resident(shape):
    zeros = (0,) * len(shape)
    return pl.BlockSpec(shape, lambda *_: zeros, pipeline_mode=pl.Buffered(1))


def _layer_norm(y, g, b):
    mu = jnp.mean(y, axis=-1, keepdims=True)
    d = y - mu
    var = jnp.mean(d * d, axis=-1, keepdims=True)
    return d * lax.rsqrt(var + LN_EPS) * g + b


def _silu(x):
    return x * jax.nn.sigmoid(x)


FFN_COL_CHUNK = 256


def _ffn_kernel(x_ref, win_ref, wout_ref, g_ref, b_ref, o_ref, act_ref):
    x = x_ref[...]
    xb = x.astype(BF16)
    for c in range(D_FF // FFN_COL_CHUNK):
        lo = c * FFN_COL_CHUNK
        gate = jnp.dot(xb, win_ref[:, lo:lo + FFN_COL_CHUNK], preferred_element_type=F32)
        up = jnp.dot(xb, win_ref[:, D_FF + lo:D_FF + lo + FFN_COL_CHUNK], preferred_element_type=F32)
        act_ref[:, lo:lo + FFN_COL_CHUNK] = (_silu(gate) * up).astype(BF16)
    y = jnp.dot(act_ref[...], wout_ref[...], preferred_element_type=F32)
    o_ref[...] = _layer_norm(DEEPNORM_ALPHA * x + FFN_HALF * y, g_ref[...], b_ref[...])


def _ffn(h, w_in, w_out, g, b, *, tm=512):
    n = h.shape[0]
    return pl.pallas_call(
        _ffn_kernel,
        grid=(n // tm,),
        in_specs=[
            pl.BlockSpec((tm, D_MODEL), lambda i: (i, 0)),
            _resident((D_MODEL, 2 * D_FF)),
            _resident((D_FF, D_MODEL)),
            _resident((1, D_MODEL)),
            _resident((1, D_MODEL)),
        ],
        out_specs=pl.BlockSpec((tm, D_MODEL), lambda i: (i, 0)),
        out_shape=jax.ShapeDtypeStruct((n, D_MODEL), F32),
        scratch_shapes=[pltpu.VMEM((tm, D_FF), BF16)],
        compiler_params=_params(("parallel",)),
        name="ffn",
    )(h, w_in, w_out, g, b)


def _proj_ln_kernel(h_ref, pre_ref, w_ref, g_ref, b_ref, o_ref, *, pre_transposed):
    contract_pre = 0 if pre_transposed else 1
    y = lax.dot_general(pre_ref[...], w_ref[...], (((contract_pre,), (0,)), ((), ())),
                        preferred_element_type=F32)
    o_ref[...] = _layer_norm(DEEPNORM_ALPHA * h_ref[...] + y, g_ref[...], b_ref[...])


def _proj_ln(h, pre, w_out, g, b, *, pre_transposed=False, tm=512):
    n = h.shape[0]
    k = w_out.shape[0]
    pre_spec = (pl.BlockSpec((k, tm), lambda i: (0, i)) if pre_transposed
                else pl.BlockSpec((tm, k), lambda i: (i, 0)))
    return pl.pallas_call(
        functools.partial(_proj_ln_kernel, pre_transposed=pre_transposed),
        grid=(n // tm,),
        in_specs=[
            pl.BlockSpec((tm, D_MODEL), lambda i: (i, 0)),
            pre_spec,
            _resident((k, D_MODEL)),
            _resident((1, D_MODEL)),
            _resident((1, D_MODEL)),
        ],
        out_specs=pl.BlockSpec((tm, D_MODEL), lambda i: (i, 0)),
        out_shape=jax.ShapeDtypeStruct((n, D_MODEL), F32),
        compiler_params=_params(("parallel",)),
        name="proj_ln",
    )(h, pre, w_out, g, b)


def _cmlp_kernel(h_ref, win_ref, lng_ref, lnb_ref, wsp_ref, bias_ref, wout_ref, g_ref, b_ref,
                 o_ref, *rest):
    *v_refs, pre_ref = rest
    tm = h_ref.shape[0]
    x = h_ref[...]
    uv = jnp.dot(x.astype(BF16), win_ref[...], preferred_element_type=F32)
    uv = 0.5 * uv * (1.0 + lax.erf(uv * math.sqrt(0.5)))
    u = uv[:, :CMLP_DV]
    v = _layer_norm(uv[:, CMLP_DV:], lng_ref[...], lnb_ref[...])
    for v_ref in v_refs:
        v_ref[...] = v
    vb = v.astype(BF16)
    row = lax.broadcasted_iota(jnp.int32, (CMLP_CHUNK, CMLP_CHUNK), 0)
    col = lax.broadcasted_iota(jnp.int32, (CMLP_CHUNK, CMLP_CHUNK), 1)
    causal = col <= row
    for g in range(CMLP_GROUPS):
        gl = g * LANES
        wg = jnp.where(causal, wsp_ref[g], 0.0).astype(BF16)
        for c in range(tm // CMLP_CHUNK):
            cl = c * CMLP_CHUNK
            s = jnp.dot(wg, vb[cl:cl + CMLP_CHUNK, gl:gl + LANES], preferred_element_type=F32)
            s = s + bias_ref[:, gl:gl + LANES]
            pre_ref[cl:cl + CMLP_CHUNK, gl:gl + LANES] = (u[cl:cl + CMLP_CHUNK, gl:gl + LANES] * s).astype(BF16)
    y = jnp.dot(pre_ref[...], wout_ref[...], preferred_element_type=F32)
    o_ref[...] = _layer_norm(DEEPNORM_ALPHA * x + y, g_ref[...], b_ref[...])


def _cmlp(h, w_in, ln_g, ln_b, w_sp, bias_full, w_out, g, b, *, emit_v, tm=256):
    n = h.shape[0]
    n_out = 2 if emit_v else 1
    return pl.pallas_call(
        _cmlp_kernel,
        grid=(n // tm,),
        in_specs=[
            pl.BlockSpec((tm, D_MODEL), lambda i: (i, 0)),
            _resident((D_MODEL, 2 * CMLP_DV)),
            _resident((1, CMLP_DV)),
            _resident((1, CMLP_DV)),
            _resident((CMLP_GROUPS, CMLP_CHUNK, CMLP_CHUNK)),
            _resident((CMLP_CHUNK, CMLP_DV)),
            _resident((CMLP_DV, D_MODEL)),
            _resident((1, D_MODEL)),
            _resident((1, D_MODEL)),
        ],
        out_specs=[pl.BlockSpec((tm, D_MODEL), lambda i: (i, 0))] * n_out,
        out_shape=[jax.ShapeDtypeStruct((n, D_MODEL), F32)] * n_out,
        scratch_shapes=[pltpu.VMEM((tm, CMLP_DV), BF16)],
        compiler_params=_params(("parallel",)),
        name="cmlp",
    )(h, w_in, ln_g, ln_b, w_sp, bias_full, w_out, g, b)


def _qkv_kernel(h_ref, w_ref, cos_ref, sin_ref, q_ref, k_ref, v_ref, *attn_refs):
    tm = h_ref.shape[0]
    qkv = jnp.dot(h_ref[...].astype(BF16), w_ref[...], preferred_element_type=F32)
    cos = cos_ref[...]
    sin = sin_ref[...]
    for hd in range(HEADS):
        lo = hd * HEAD_DIM
        qh = qkv[:, lo:lo + HEAD_DIM]
        q_ref[:, lo:lo + HEAD_DIM] = (qh * cos + pltpu.roll(qh, HEAD_DIM // 2, axis=1) * sin).astype(BF16)
        kh = qkv[:, D_MODEL + lo:D_MODEL + lo + HEAD_DIM]
        k_ref[:, lo:lo + HEAD_DIM] = kh * cos + pltpu.roll(kh, HEAD_DIM // 2, axis=1) * sin
    v = qkv[:, 2 * D_MODEL:]
    v_ref[...] = v
    if attn_refs:
        kb_ref, vt_ref, km_ref = attn_refs
        kb_ref[...] = k_ref[...].astype(BF16)
        vt_ref[...] = v.T.astype(BF16)
        for j in range(tm // MOBA_BLOCK):
            blk = k_ref[j * MOBA_BLOCK:(j + 1) * MOBA_BLOCK, :]
            km_ref[0, j:j + 1, :] = jnp.sum(blk, axis=0, keepdims=True) * (1.0 / MOBA_BLOCK)


def _qkv(h, w_qkv, cos, sin, *, for_prompt, tm=512):
    n = h.shape[0]
    nkb = tm // MOBA_BLOCK
    row_spec = lambda width: pl.BlockSpec((tm, width), lambda i: (i, 0))
    out_specs = [row_spec(D_MODEL)] * 3
    out_shape = [
        jax.ShapeDtypeStruct((n, D_MODEL), BF16),
        jax.ShapeDtypeStruct((n, D_MODEL), F32),
        jax.ShapeDtypeStruct((n, D_MODEL), F32),
    ]
    if for_prompt:
        out_specs += [
            row_spec(D_MODEL),
            pl.BlockSpec((D_MODEL, tm), lambda i: (0, i)),
            pl.BlockSpec((1, nkb, D_MODEL), lambda i: (i, 0, 0)),
        ]
        out_shape += [
            jax.ShapeDtypeStruct((n, D_MODEL), BF16),
            jax.ShapeDtypeStruct((D_MODEL, n), BF16),
            jax.ShapeDtypeStruct((n // tm, nkb, D_MODEL), F32),
        ]
    return pl.pallas_call(
        _qkv_kernel,
        grid=(n // tm,),
        in_specs=[
            row_spec(D_MODEL),
            _resident((D_MODEL, 3 * D_MODEL)),
            row_spec(HEAD_DIM),
            row_spec(HEAD_DIM),
        ],
        out_specs=out_specs,
        out_shape=out_shape,
        compiler_params=_params(("parallel",)),
        name="qkv_rope",
    )(h, w_qkv, cos, sin)


def _top3_sublanes(gate, n_valid):
    n_rows = gate.shape[0]
    blk = lax.broadcasted_iota(jnp.int32, gate.shape, 0)
    gate = jnp.where(blk < n_valid, gate, NEG_INF)
    blk = blk.astype(F32)
    bias = jnp.full(gate.shape, NEG_INF, F32)
    for _ in range(MOBA_TOPK):
        m = jnp.max(gate, axis=0, keepdims=True)
        first = jnp.min(jnp.where(gate == m, blk, float(n_rows)), axis=0, keepdims=True)
        pick = blk == first
        bias = jnp.where(pick & (m > NEG_INF), 0.0, bias)
        gate = jnp.where(pick, NEG_INF, gate)
    return bias


def _top3_lanes(gate, n_valid):
    lane = lax.broadcasted_iota(jnp.int32, gate.shape, 1)
    gate = jnp.where(lane < n_valid, gate, NEG_INF)
    bias = jnp.full(gate.shape, NEG_INF, F32)
    for _ in range(MOBA_TOPK):
        m = jnp.max(gate, axis=1, keepdims=True)
        first = jnp.min(jnp.where(gate == m, lane, LANES), axis=1, keepdims=True)
        pick = lane == first
        bias = jnp.where(pick & (m > NEG_INF), 0.0, bias)
        gate = jnp.where(pick, NEG_INF, gate)
    return bias


def _attn_prompt_kernel(q_ref, k_ref, vt_ref, km_ref, o_ref, bias_ref, acc_ref):
    i = pl.program_id(1)
    tq = q_ref.shape[0]
    c_exp = HEAD_DIM ** -0.5 * math.log2(math.e)
    nt = (((1,), (1,)), ((), ()))

    def scores(h, start):
        lo = h * HEAD_DIM
        kj = k_ref[0, pl.ds(start, MOBA_BLOCK), lo:lo + HEAD_DIM]
        return lax.dot_general(kj, q_ref[:, lo:lo + HEAD_DIM], nt, preferred_element_type=F32)

    def weighted_values(h, start, p):
        lo = h * HEAD_DIM
        return jnp.dot(vt_ref[lo:lo + HEAD_DIM, pl.ds(start, MOBA_BLOCK)], p.astype(BF16),
                       preferred_element_type=F32)

    own = pl.multiple_of(i * MOBA_BLOCK, MOBA_BLOCK)
    key = lax.broadcasted_iota(jnp.int32, (MOBA_BLOCK, tq), 0)
    qry = lax.broadcasted_iota(jnp.int32, (MOBA_BLOCK, tq), 1)
    gates = [
        lax.dot_general(km_ref[0, :, h * HEAD_DIM:(h + 1) * HEAD_DIM],
                        q_ref[:, h * HEAD_DIM:(h + 1) * HEAD_DIM].astype(F32), nt,
                        precision=HIGHEST, preferred_element_type=F32)
        for h in range(HEADS)
    ]
    own_scores = [scores(h, own) for h in range(HEADS)]
    ms, ls, ps = [], [], []
    for h in range(HEADS):
        bias_ref[h] = _top3_sublanes(gates[h], i)
        s = jnp.where(key <= qry, own_scores[h], NEG_INF)
        m = jnp.max(s, axis=0, keepdims=True)
        p = jnp.exp2((s - m) * c_exp)
        ms.append(m)
        ls.append(jnp.sum(p, axis=0, keepdims=True))
        ps.append(p)
    for h in range(HEADS):
        acc_ref[h] = weighted_values(h, own, ps[h])

    def body(j, carry):
        ms, ls = carry
        start = pl.multiple_of(j * MOBA_BLOCK, MOBA_BLOCK)
        ss = [scores(h, start) for h in range(HEADS)]
        new_ms, new_ls, alphas, ps = [], [], [], []
        for h in range(HEADS):
            s = ss[h] + bias_ref[h, pl.ds(j, 1), :]
            m_new = jnp.maximum(ms[h], jnp.max(s, axis=0, keepdims=True))
            alpha = jnp.exp2((ms[h] - m_new) * c_exp)
            p = jnp.exp2((s - m_new) * c_exp)
            new_ms.append(m_new)
            new_ls.append(alpha * ls[h] + jnp.sum(p, axis=0, keepdims=True))
            alphas.append(alpha)
            ps.append(p.astype(BF16))
        for h in range(HEADS):
            acc_ref[h] = alphas[h] * acc_ref[h] + weighted_values(h, start, ps[h])
        return tuple(new_ms), tuple(new_ls)

    _, ls = lax.fori_loop(0, i, body, (tuple(ms), tuple(ls)))
    for h in range(HEADS):
        o_ref[h * HEAD_DIM:(h + 1) * HEAD_DIM, :] = (acc_ref[h] / ls[h]).astype(BF16)


def _attn_prompt(q, k, vt, kmean):
    bn, t, _ = k.shape
    nb = t // MOBA_BLOCK
    tq = MOBA_BLOCK
    nq = t // tq
    return pl.pallas_call(
        _attn_prompt_kernel,
        grid=(bn, nq),
        in_specs=[
            pl.BlockSpec((tq, D_MODEL), lambda b, i: (b * nq + i, 0)),
            pl.BlockSpec((1, t, D_MODEL), lambda b, i: (b, 0, 0), pipeline_mode=pl.Buffered(1)),
            pl.BlockSpec((D_MODEL, t), lambda b, i: (0, b), pipeline_mode=pl.Buffered(1)),
            pl.BlockSpec((1, nb, D_MODEL), lambda b, i: (b, 0, 0)),
        ],
        out_specs=pl.BlockSpec((D_MODEL, tq), lambda b, i: (0, b * nq + i)),
        out_shape=jax.ShapeDtypeStruct((D_MODEL, bn * t), BF16),
        scratch_shapes=[
            pltpu.VMEM((HEADS, nb, tq), F32),
            pltpu.VMEM((HEADS, HEAD_DIM, tq), F32),
        ],
        compiler_params=_params(("parallel", "arbitrary")),
        name="moba_prompt",
    )(q, k, vt, kmean)


def _attn_sample_kernel(pt_ref, q_ref, kn_ref, vn_ref, *rest, n_pages):
    kp_refs = rest[:n_pages]
    vp_refs = rest[n_pages:2 * n_pages]
    o_ref, kb_ref, vb_ref, s_ref = rest[2 * n_pages:]
    del pt_ref
    tq = q_ref.shape[1]
    past = n_pages * PAGE_SIZE
    n_past_blocks = past // MOBA_BLOCK
    n_new = 2 * SUBLANES
    c_exp = HEAD_DIM ** -0.5 * math.log2(math.e)

    q = q_ref[0].astype(F32)
    qt = jnp.concatenate([q] * (LANES // tq), axis=0)
    rid = lax.broadcasted_iota(jnp.int32, (LANES, D_MODEL), 0)
    lid = lax.broadcasted_iota(jnp.int32, (LANES, D_MODEL), 1)
    qr = jnp.where((rid < HEADS * tq) & (lid // HEAD_DIM == rid // tq), qt, 0.0).astype(BF16)

    nt = (((1,), (1,)), ((), ()))
    for p in range(n_pages):
        rows = slice(p * PAGE_SIZE, (p + 1) * PAGE_SIZE)
        for hd in range(HEADS):
            lo = hd * HEAD_DIM
            head_rows = pl.ds(hd, PAGE_SIZE, stride=HEADS)
            kb_ref[rows, lo:lo + HEAD_DIM] = kp_refs[p][0, head_rows, :].astype(BF16)
            vb_ref[rows, lo:lo + HEAD_DIM] = vp_refs[p][0, head_rows, :].astype(BF16)
        s_ref[:, rows] = lax.dot_general(qr, kb_ref[rows, :], nt, preferred_element_type=F32)
    zpad = jnp.zeros((n_new - tq, D_MODEL), F32)
    kb_ref[past:past + n_new, :] = jnp.concatenate([kn_ref[0], zpad], axis=0).astype(BF16)
    vb_ref[past:past + n_new, :] = jnp.concatenate([vn_ref[0], zpad], axis=0).astype(BF16)

    lane = lax.broadcasted_iota(jnp.int32, (LANES, LANES), 1)
    gate = jnp.full((LANES, LANES), NEG_INF, F32)
    for n in range(n_past_blocks):
        lo = n * MOBA_BLOCK
        block_mean = jnp.sum(s_ref[:, lo:lo + MOBA_BLOCK], axis=1, keepdims=True) * (1.0 / MOBA_BLOCK)
        gate = jnp.where(lane == n, block_mean, gate)
    bias = _top3_lanes(gate, n_past_blocks)

    s_new = lax.dot_general(qr, kb_ref[past:past + n_new, :], nt, preferred_element_type=F32)
    qi = lax.broadcasted_iota(jnp.int32, (LANES, n_new), 0) % tq
    kr = lax.broadcasted_iota(jnp.int32, (LANES, n_new), 1)
    s_new = jnp.where(kr <= qi, s_new, NEG_INF)
    m = jnp.max(s_new, axis=1, keepdims=True)
    for n in range(n_past_blocks):
        lo = n * MOBA_BLOCK
        sb = s_ref[:, lo:lo + MOBA_BLOCK] + bias[:, n:n + 1]
        s_ref[:, lo:lo + MOBA_BLOCK] = sb
        m = jnp.maximum(m, jnp.max(sb, axis=1, keepdims=True))
    p_new = jnp.exp2((s_new - m) * c_exp)
    l = jnp.sum(p_new, axis=1, keepdims=True)
    for n in range(n_past_blocks):
        lo = n * MOBA_BLOCK
        pb = jnp.exp2((s_ref[:, lo:lo + MOBA_BLOCK] - m) * c_exp)
        s_ref[:, lo:lo + MOBA_BLOCK] = pb
        l = l + jnp.sum(pb, axis=1, keepdims=True)
    inv_l = 1.0 / l
    out = jnp.dot((s_ref[:, :past] * inv_l).astype(BF16), vb_ref[:past, :], preferred_element_type=F32)
    out = out + jnp.dot((p_new * inv_l).astype(BF16), vb_ref[past:past + n_new, :], preferred_element_type=F32)
    for hd in range(HEADS):
        lo = hd * HEAD_DIM
        o_ref[0, :, lo:lo + HEAD_DIM] = out[hd * tq:(hd + 1) * tq, lo:lo + HEAD_DIM].astype(BF16)


def _attn_sample(q, k_new, v_new, cache_k, cache_v, page_table):
    ns, tq, _ = q.shape
    n_pages = page_table.shape[1]
    past = n_pages * PAGE_SIZE
    n_new = 2 * SUBLANES
    assert tq == SUBLANES and HEADS * tq <= LANES and past % MOBA_BLOCK == 0

    def page_spec(p):
        return pl.BlockSpec((1, PAGE_SIZE * HEADS, HEAD_DIM), lambda s, pt: (pt[s * n_pages + p], 0, 0))

    seq_spec = pl.BlockSpec((1, tq, D_MODEL), lambda s, pt: (s, 0, 0))
    grid_spec = pltpu.PrefetchScalarGridSpec(
        num_scalar_prefetch=1,
        grid=(ns,),
        in_specs=[seq_spec, seq_spec, seq_spec]
        + [page_spec(p) for p in range(n_pages)]
        + [page_spec(p) for p in range(n_pages)],
        out_specs=seq_spec,
        scratch_shapes=[
            pltpu.VMEM((past + n_new, D_MODEL), BF16),
            pltpu.VMEM((past + n_new, D_MODEL), BF16),
            pltpu.VMEM((LANES, past), F32),
        ],
    )
    return pl.pallas_call(
        functools.partial(_attn_sample_kernel, n_pages=n_pages),
        grid_spec=grid_spec,
        out_shape=jax.ShapeDtypeStruct((ns, tq, D_MODEL), BF16),
        compiler_params=_params(("arbitrary",)),
        name="moba_sample",
    )(page_table.reshape(-1), q, k_new, v_new, *([cache_k] * n_pages), *([cache_v] * n_pages))


def _ssm_in_kernel(h_ref, wz_ref, wx_ref, wdt_ref, z_ref, raw_ref, dt_ref):
    xb = h_ref[...].astype(BF16)
    z_ref[...] = jnp.dot(xb, wz_ref[...], preferred_element_type=F32)
    raw_ref[...] = jnp.dot(xb, wx_ref[...], preferred_element_type=F32)
    dt_ref[...] = jnp.dot(xb, wdt_ref[...], preferred_element_type=F32)


def _ssm_in(h, w_z, w_x, w_dt, *, tm=256):
    n = h.shape[0]
    row_spec = lambda width: pl.BlockSpec((tm, width), lambda i: (i, 0))
    return pl.pallas_call(
        _ssm_in_kernel,
        grid=(n // tm,),
        in_specs=[
            row_spec(D_MODEL),
            _resident((D_MODEL, SSM_D_INNER)),
            _resident((D_MODEL, SSM_CONV_DIM)),
            _resident((D_MODEL, LANES)),
        ],
        out_specs=[row_spec(SSM_D_INNER), row_spec(SSM_CONV_DIM), row_spec(LANES)],
        out_shape=[
            jax.ShapeDtypeStruct((n, SSM_D_INNER), F32),
            jax.ShapeDtypeStruct((n, SSM_CONV_DIM), F32),
            jax.ShapeDtypeStruct((n, LANES), F32),
        ],
        compiler_params=_params(("parallel",)),
        name="ssm_in",
    )(h, w_z, w_x, w_dt)


def _ssd_kernel(raw_ref, z_ref, dtr_ref, conv0_ref, ssm0_ref, wconv_ref, bconv_ref, dtb_ref, alog_ref,
                dskip_ref, ng_ref, tri_ref,
                y_ref, convo_ref, ssmo_ref, ext_ref, state_ref, yacc_ref):
    c = pl.program_id(1)
    t = raw_ref.shape[0]
    L = tri_ref.shape[0]
    halo = SUBLANES

    @pl.when(c == 0)
    def _():
        ext_ref[0:halo, :] = conv0_ref[0]
        state_ref[...] = ssm0_ref[0]

    ext_ref[halo:halo + t, :] = raw_ref[...]
    if t < L:
        ext_ref[halo + t:halo + L, :] = jnp.zeros((L - t, SSM_CONV_DIM), F32)
    ext = ext_ref[...]
    conv = bconv_ref[...] + wconv_ref[SSM_CONV - 1:SSM_CONV, :] * ext[halo:, :]
    for back in range(1, SSM_CONV):
        w = SSM_CONV - 1 - back
        conv = conv + wconv_ref[w:w + 1, :] * pltpu.roll(ext, back, axis=0)[halo:, :]
    tail = ext_ref[t:t + halo, :]
    ext_ref[0:halo, :] = tail
    convo_ref[0] = tail
    xbc = _silu(conv)
    x = xbc[:, :SSM_D_INNER]
    bm = xbc[:, SSM_D_INNER:SSM_D_INNER + SSM_BC_DIM].astype(BF16)
    cm = xbc[:, SSM_D_INNER + SSM_BC_DIM:]

    dtx = dtr_ref[...] + dtb_ref[...]
    dt = jnp.maximum(dtx, 0.0) + jnp.log(1.0 + jnp.exp(-jnp.abs(dtx)))
    if t < L:
        dt = jnp.concatenate([dt, jnp.zeros((L - t, LANES), F32)], axis=0)
    a = -jnp.exp(alog_ref[...])
    adt = dt * a
    acs = jnp.dot(tri_ref[...], adt, precision=HIGHEST, preferred_element_type=F32)
    acs_t = acs.T
    dt_t = dt.T
    w_t = jnp.exp(acs_t[:, L - 1:L] - acs_t) * dt_t
    x_t = x.T
    xb = x.astype(BF16)

    row = lax.broadcasted_iota(jnp.int32, (L, L), 0)
    col = lax.broadcasted_iota(jnp.int32, (L, L), 1)
    causal = col <= row
    nt = (((1,), (1,)), ((), ()))
    for g in range(SSM_GROUPS):
        bg = bm[:, g * SSM_D_STATE:(g + 1) * SSM_D_STATE]
        cg = cm[:, g * SSM_D_STATE:(g + 1) * SSM_D_STATE]
        cb = lax.dot_general(cg.astype(BF16), bg, nt, preferred_element_type=F32)
        for e in range(SSM_HPG):
            hd = g * SSM_HPG + e
            lo = hd * SSM_HEAD_DIM
            acs_col = jnp.broadcast_to(acs[:, hd:hd + 1], (L, SSM_D_STATE))
            lmat = jnp.where(causal, jnp.exp(acs_col[:, :L] - acs_t[hd:hd + 1, :]), 0.0)
            mh = (cb * lmat * dt_t[hd:hd + 1, :]).astype(BF16)
            ch = (cg * jnp.exp(acs_col)).astype(BF16)
            sh = state_ref[lo:lo + SSM_HEAD_DIM, :]
            yh = jnp.dot(mh, xb[:, lo:lo + SSM_HEAD_DIM], preferred_element_type=F32)
            yh = yh + lax.dot_general(ch, sh.astype(BF16), nt, preferred_element_type=F32)
            yacc_ref[:, lo:lo + SSM_HEAD_DIM] = yh
            xw = (x_t[lo:lo + SSM_HEAD_DIM, :] * w_t[hd:hd + 1, :]).astype(BF16)
            upd = jnp.dot(xw, bg, preferred_element_type=F32)
            state_ref[lo:lo + SSM_HEAD_DIM, :] = jnp.exp(acs_t[hd:hd + 1, L - 1:L]) * sh + upd
    ssmo_ref[0] = state_ref[...]

    y = yacc_ref[0:t, :] + x[:t] * dskip_ref[...]
    y = y * _silu(z_ref[...])
    gw = SSM_D_INNER // SSM_GROUPS
    for g in range(SSM_GROUPS):
        yg = y[:, g * gw:(g + 1) * gw]
        yg = yg * lax.rsqrt(jnp.mean(yg * yg, axis=-1, keepdims=True) + RMS_EPS)
        y_ref[:, g * gw:(g + 1) * gw] = (yg * ng_ref[:, g * gw:(g + 1) * gw]).astype(y_ref.dtype)


def _ssd(raw, z, dt_raw, conv0, ssm0, w_conv, b_conv, dt_bias, a_log, d_skip, norm_g):
    bn, t_total, _ = raw.shape
    t = min(t_total, SSM_CHUNK)
    nc = t_total // t
    chunk = max(t, 2 * SUBLANES)
    tri = jnp.tril(jnp.ones((chunk, chunk), F32))
    blk = lambda width: pl.BlockSpec((None, t, width), lambda b, c: (b, c, 0))
    per_seq = lambda rows, width: pl.BlockSpec((1, rows, width), lambda b, c: (b, 0, 0))
    return pl.pallas_call(
        _ssd_kernel,
        grid=(bn, nc),
        in_specs=[
            blk(SSM_CONV_DIM), blk(SSM_D_INNER), blk(LANES),
            per_seq(SUBLANES, SSM_CONV_DIM),
            per_seq(SSM_D_INNER, SSM_D_STATE),
            _resident((SUBLANES, SSM_CONV_DIM)),
            _resident((1, SSM_CONV_DIM)),
            _resident((1, LANES)),
            _resident((1, LANES)),
            _resident((1, SSM_D_INNER)),
            _resident((1, SSM_D_INNER)),
            _resident((chunk, chunk)),
        ],
        out_specs=[
            blk(SSM_D_INNER),
            per_seq(SUBLANES, SSM_CONV_DIM),
            per_seq(SSM_D_INNER, SSM_D_STATE),
        ],
        out_shape=[
            jax.ShapeDtypeStruct((bn, t_total, SSM_D_INNER), BF16),
            jax.ShapeDtypeStruct((bn, SUBLANES, SSM_CONV_DIM), F32),
            jax.ShapeDtypeStruct((bn, SSM_D_INNER, SSM_D_STATE), F32),
        ],
        scratch_shapes=[
            pltpu.VMEM((SUBLANES + chunk, SSM_CONV_DIM), F32),
            pltpu.VMEM((SSM_D_INNER, SSM_D_STATE), F32),
            pltpu.VMEM((chunk, SSM_D_INNER), F32),
        ],
        compiler_params=_params(("arbitrary", "arbitrary")),
        name="ssd",
    )(raw, z, dt_raw, conv0, ssm0, w_conv, b_conv, dt_bias, a_log, d_skip, norm_g, tri)


def _rope_tables(pos):
    half = HEAD_DIM // 2
    inv = ROPE_THETA ** (-jnp.arange(half, dtype=F32) / half)
    ang = pos.astype(F32)[:, None] * inv[None, :]
    cos, sin = jnp.cos(ang), jnp.sin(ang)
    return jnp.concatenate([cos, cos], axis=-1), jnp.concatenate([-sin, sin], axis=-1)


def _row(v):
    return v.reshape(1, -1)


def kernel(x_prompt, x_sample, cache_k, cache_v, state_conv, state_ssm, page_table, ln_g, ln_b, ffn_w_in, ffn_w_out, cmlp_w_in, cmlp_ln_g, cmlp_ln_b, cmlp_w_s, cmlp_b_s, cmlp_w_out, moba_w_qkv, moba_w_out, ssm_w_in, ssm_w_conv, ssm_b_conv, ssm_dt_bias, ssm_a_log, ssm_d, ssm_norm_g, ssm_w_out):
    n_prompt, t_prompt, _ = x_prompt.shape
    n_sample, t_sample, _ = x_sample.shape
    n_pages = page_table.shape[1]
    past_len = n_pages * PAGE_SIZE
    hp = x_prompt.reshape(n_prompt * t_prompt, D_MODEL)
    hs = x_sample.reshape(n_sample * t_sample, D_MODEL)

    cos_p, sin_p = _rope_tables(jnp.arange(t_prompt))
    cos_p, sin_p = jnp.tile(cos_p, (n_prompt, 1)), jnp.tile(sin_p, (n_prompt, 1))
    cos_s, sin_s = _rope_tables(past_len + jnp.arange(t_sample))
    cos_s, sin_s = jnp.tile(cos_s, (n_sample, 1)), jnp.tile(sin_s, (n_sample, 1))

    outs = {k: [] for k in ("cmlp_v", "k_p", "v_p", "k_s", "v_s", "conv_p", "ssm_p", "conv_s", "ssm_s")}
    for i in range(DEPTH):
        w_in, w_out = ffn_w_in[i, 0].astype(BF16), ffn_w_out[i, 0].astype(BF16)
        hp = _ffn(hp, w_in, w_out, _row(ln_g[i, 0]), _row(ln_b[i, 0]))
        hs = _ffn(hs, w_in, w_out, _row(ln_g[i, 0]), _row(ln_b[i, 0]))
        kind, j = i % 3, i // 3
        g1, b1 = _row(ln_g[i, 1]), _row(ln_b[i, 1])
        if kind == 0:
            samples_per_chunk = CMLP_CHUNK // t_sample
            w_sp_s = jnp.einsum("ab,gts->gatbs", jnp.eye(samples_per_chunk, dtype=F32),
                                cmlp_w_s[j][:, :t_sample, :t_sample]).reshape(CMLP_GROUPS, CMLP_CHUNK, CMLP_CHUNK)
            bias_p = jnp.repeat(cmlp_b_s[j].T, CMLP_DV // CMLP_GROUPS, axis=1)
            bias_s = jnp.tile(jnp.repeat(cmlp_b_s[j][:, :t_sample].T, CMLP_DV // CMLP_GROUPS, axis=1),
                              (samples_per_chunk, 1))
            a_par = (cmlp_w_in[j].astype(BF16), _row(cmlp_ln_g[j]), _row(cmlp_ln_b[j]))
            w_o = cmlp_w_out[j].astype(BF16)
            (hp,) = _cmlp(hp, *a_par, cmlp_w_s[j], bias_p, w_o, g1, b1, emit_v=False)
            hs, v_rows = _cmlp(hs, *a_par, w_sp_s, bias_s, w_o, g1, b1, emit_v=True)
            outs["cmlp_v"].append(v_rows.reshape(n_sample, t_sample, CMLP_DV))
        elif kind == 1:
            w_qkv = moba_w_qkv[j].astype(BF16)
            w_o = moba_w_out[j].astype(BF16)
            q, k, v, kb, vt, kmean = _qkv(hp, w_qkv, cos_p, sin_p, for_prompt=True)
            o_t = _attn_prompt(q, kb.reshape(n_prompt, t_prompt, D_MODEL), vt,
                               kmean.reshape(n_prompt, t_prompt // MOBA_BLOCK, D_MODEL))
            hp = _proj_ln(hp, o_t, w_o, g1, b1, pre_transposed=True)
            outs["k_p"].append(k.reshape(n_prompt, t_prompt, HEADS, HEAD_DIM))
            outs["v_p"].append(v.reshape(n_prompt, t_prompt, HEADS, HEAD_DIM))
            q, k, v = _qkv(hs, w_qkv, cos_s, sin_s, for_prompt=False)
            shs = (n_sample, t_sample, D_MODEL)
            pool = cache_k.shape[1]
            o = _attn_sample(q.reshape(shs), k.reshape(shs), v.reshape(shs),
                             cache_k.reshape(-1, PAGE_SIZE * HEADS, HEAD_DIM),
                             cache_v.reshape(-1, PAGE_SIZE * HEADS, HEAD_DIM), page_table + j * pool)
            hs = _proj_ln(hs, o.reshape(-1, D_MODEL), w_o, g1, b1)
            outs["k_s"].append(k.reshape(n_sample, t_sample, HEADS, HEAD_DIM))
            outs["v_s"].append(v.reshape(n_sample, t_sample, HEADS, HEAD_DIM))
        else:
            w = ssm_w_in[j]
            w_z = w[:, :SSM_D_INNER].astype(BF16)
            w_x = w[:, SSM_D_INNER:SSM_D_INNER + SSM_CONV_DIM].astype(BF16)
            w_dt = jnp.pad(w[:, SSM_D_INNER + SSM_CONV_DIM:], ((0, 0), (0, LANES - SSM_HEADS))).astype(BF16)
            c_par = (
                jnp.pad(ssm_w_conv[j], ((0, SUBLANES - SSM_CONV), (0, 0))),
                _row(ssm_b_conv[j]),
                _row(jnp.pad(ssm_dt_bias[j], (0, LANES - SSM_HEADS))),
                _row(jnp.pad(ssm_a_log[j], (0, LANES - SSM_HEADS))),
                _row(jnp.repeat(ssm_d[j], SSM_HEAD_DIM)),
                _row(ssm_norm_g[j]),
            )
            w_o = ssm_w_out[j].astype(BF16)
            halo_pad = ((0, 0), (SUBLANES - (SSM_CONV - 1), 0), (0, 0))

            z, raw, dt_raw = _ssm_in(hp, w_z, w_x, w_dt)
            y, conv_t, ssm_f = _ssd(
                raw.reshape(n_prompt, t_prompt, -1), z.reshape(n_prompt, t_prompt, -1),
                dt_raw.reshape(n_prompt, t_prompt, -1),
                jnp.zeros((n_prompt, SUBLANES, SSM_CONV_DIM), F32),
                jnp.zeros((n_prompt, SSM_D_INNER, SSM_D_STATE), F32), *c_par)
            hp = _proj_ln(hp, y.reshape(-1, SSM_D_INNER), w_o, g1, b1)
            outs["conv_p"].append(conv_t[:, SUBLANES - (SSM_CONV - 1):])
            outs["ssm_p"].append(ssm_f.reshape(n_prompt, SSM_HEADS, SSM_HEAD_DIM, SSM_D_STATE))

            z, raw, dt_raw = _ssm_in(hs, w_z, w_x, w_dt)
            y, conv_t, ssm_f = _ssd(
                raw.reshape(n_sample, t_sample, -1), z.reshape(n_sample, t_sample, -1),
                dt_raw.reshape(n_sample, t_sample, -1),
                jnp.pad(state_conv[j], halo_pad),
                state_ssm[j].reshape(n_sample, SSM_D_INNER, SSM_D_STATE), *c_par)
            hs = _proj_ln(hs, y.reshape(-1, SSM_D_INNER), w_o, g1, b1)
            outs["conv_s"].append(conv_t[:, SUBLANES - (SSM_CONV - 1):])
            outs["ssm_s"].append(ssm_f.reshape(n_sample, SSM_HEADS, SSM_HEAD_DIM, SSM_D_STATE))
        w_in, w_out = ffn_w_in[i, 1].astype(BF16), ffn_w_out[i, 1].astype(BF16)
        hp = _ffn(hp, w_in, w_out, _row(ln_g[i, 2]), _row(ln_b[i, 2]))
        hs = _ffn(hs, w_in, w_out, _row(ln_g[i, 2]), _row(ln_b[i, 2]))
    st = lambda name: jnp.stack(outs[name])
    return (hp.reshape(n_prompt, t_prompt, D_MODEL), hs.reshape(n_sample, t_sample, D_MODEL),
            st("cmlp_v"), st("k_p"), st("v_p"), st("k_s"), st("v_s"),
            st("conv_p"), st("ssm_p"), st("conv_s"), st("ssm_s"))
```

```python
import functools
import math

import jax
import jax.numpy as jnp
from jax import lax
from jax.experimental import pallas as pl
from jax.experimental.pallas import tpu as pltpu

F32 = jnp.float32
BF16 = jnp.bfloat16
HIGHEST = lax.Precision.HIGHEST

LANES = 128
SUBLANES = 8
VMEM_LIMIT_BYTES = 56 * 1024 * 1024

DEPTH = 4
D_MODEL = 1024
D_FF = 2816
DEEPNORM_ALPHA = (2 * DEPTH) ** 0.25
LN_EPS = 1e-5
RMS_EPS = 1e-5
FFN_HALF = 0.5

CMLP_CHUNK = 128
CMLP_GROUPS = 8
CMLP_DV = D_MODEL

HEAD_DIM = 128
HEADS = D_MODEL // HEAD_DIM
MOBA_BLOCK = 256
MOBA_TOPK = 3
PAGE_SIZE = 128
ROPE_THETA = 10000.0

SSM_D_INNER = 2 * D_MODEL
SSM_HEAD_DIM = 64
SSM_HEADS = SSM_D_INNER // SSM_HEAD_DIM
SSM_GROUPS = 8
SSM_HPG = SSM_HEADS // SSM_GROUPS
SSM_D_STATE = 128
SSM_CONV = 4
SSM_BC_DIM = SSM_GROUPS * SSM_D_STATE
SSM_CONV_DIM = SSM_D_INNER + 2 * SSM_BC_DIM
SSM_CHUNK = 128
SSD_SAMPLE_SEQS = 4

NEG_INF = float("-inf")


def _params(semantics):
    return pltpu.CompilerParams(dimension_semantics=semantics, vmem_limit_bytes=VMEM_LIMIT_BYTES)


def _resident(shape):
    zeros = (0,) * len(shape)
    return pl.BlockSpec(shape, lambda *_: zeros, pipeline_mode=pl.Buffered(1))


def _layer_norm(y, g, b):
    mu = jnp.mean(y, axis=-1, keepdims=True)
    d = y - mu
    var = jnp.mean(d * d, axis=-1, keepdims=True)
    return d * lax.rsqrt(var + LN_EPS) * g + b


def _silu(x):
    return x * jax.nn.sigmoid(x)


FFN_COL_CHUNK = 256


def _ffn_tile(x_ref, win_ref, wout_ref, g_ref, b_ref, o_ref, act_ref):
    x = x_ref[...]
    xb = x.astype(BF16)
    for c in range(D_FF // FFN_COL_CHUNK):
        lo = c * FFN_COL_CHUNK
        gate = jnp.dot(xb, win_ref[:, lo:lo + FFN_COL_CHUNK], preferred_element_type=F32)
        up = jnp.dot(xb, win_ref[:, D_FF + lo:D_FF + lo + FFN_COL_CHUNK], preferred_element_type=F32)
        act_ref[:, lo:lo + FFN_COL_CHUNK] = (_silu(gate) * up).astype(BF16)
    y = jnp.dot(act_ref[...], wout_ref[...], preferred_element_type=F32)
    o_ref[...] = _layer_norm(DEEPNORM_ALPHA * x + FFN_HALF * y, g_ref[...], b_ref[...])


def _ffn_kernel(xp_ref, xs_ref, win_ref, wout_ref, g_ref, b_ref, op_ref, os_ref, act_ref, *, n_prompt_tiles):
    i = pl.program_id(0)

    @pl.when(i < n_prompt_tiles)
    def _():
        _ffn_tile(xp_ref, win_ref, wout_ref, g_ref, b_ref, op_ref, act_ref)

    @pl.when(i >= n_prompt_tiles)
    def _():
        _ffn_tile(xs_ref, win_ref, wout_ref, g_ref, b_ref, os_ref, act_ref)


def _ffn(hp, hs, w_in, w_out, g, b, *, tm=512):
    npt, nst = hp.shape[0] // tm, hs.shape[0] // tm
    prompt_spec = pl.BlockSpec((tm, D_MODEL), lambda i: (jnp.minimum(i, npt - 1), 0))
    sample_spec = pl.BlockSpec((tm, D_MODEL), lambda i: (jnp.maximum(i - npt, 0), 0))
    return pl.pallas_call(
        functools.partial(_ffn_kernel, n_prompt_tiles=npt),
        grid=(npt + nst,),
        in_specs=[
            prompt_spec,
            sample_spec,
            _resident((D_MODEL, 2 * D_FF)),
            _resident((D_FF, D_MODEL)),
            _resident((1, D_MODEL)),
            _resident((1, D_MODEL)),
        ],
        out_specs=[prompt_spec, sample_spec],
        out_shape=[jax.ShapeDtypeStruct(hp.shape, F32), jax.ShapeDtypeStruct(hs.shape, F32)],
        scratch_shapes=[pltpu.VMEM((tm, D_FF), BF16)],
        compiler_params=_params(("arbitrary",)),
        name="ffn",
    )(hp, hs, w_in, w_out, g, b)


def _proj_ln_kernel(h_ref, pre_ref, w_ref, g_ref, b_ref, o_ref, *, pre_transposed):
    contract_pre = 0 if pre_transposed else 1
    y = lax.dot_general(pre_ref[...], w_ref[...], (((contract_pre,), (0,)), ((), ())),
                        preferred_element_type=F32)
    o_ref[...] = _layer_norm(DEEPNORM_ALPHA * h_ref[...] + y, g_ref[...], b_ref[...])


def _proj_ln(h, pre, w_out, g, b, *, pre_transposed=False, tm=512):
    n = h.shape[0]
    k = w_out.shape[0]
    pre_spec = (pl.BlockSpec((k, tm), lambda i: (0, i)) if pre_transposed
                else pl.BlockSpec((tm, k), lambda i: (i, 0)))
    return pl.pallas_call(
        functools.partial(_proj_ln_kernel, pre_transposed=pre_transposed),
        grid=(n // tm,),
        in_specs=[
            pl.BlockSpec((tm, D_MODEL), lambda i: (i, 0)),
            pre_spec,
            _resident((k, D_MODEL)),
            _resident((1, D_MODEL)),
            _resident((1, D_MODEL)),
        ],
        out_specs=pl.BlockSpec((tm, D_MODEL), lambda i: (i, 0)),
        out_shape=jax.ShapeDtypeStruct((n, D_MODEL), F32),
        compiler_params=_params(("parallel",)),
        name="proj_ln",
    )(h, pre, w_out, g, b)


def _cmlp_kernel(h_ref, win_ref, lng_ref, lnb_ref, wsp_ref, bias_ref, wout_ref, g_ref, b_ref,
                 o_ref, *rest):
    *v_refs, pre_ref = rest
    tm = h_ref.shape[0]
    x = h_ref[...]
    uv = jnp.dot(x.astype(BF16), win_ref[...], preferred_element_type=F32)
    uv = 0.5 * uv * (1.0 + lax.erf(uv * math.sqrt(0.5)))
    u = uv[:, :CMLP_DV]
    v = _layer_norm(uv[:, CMLP_DV:], lng_ref[...], lnb_ref[...])
    for v_ref in v_refs:
        v_ref[...] = v
    vb = v.astype(BF16)
    row = lax.broadcasted_iota(jnp.int32, (CMLP_CHUNK, CMLP_CHUNK), 0)
    col = lax.broadcasted_iota(jnp.int32, (CMLP_CHUNK, CMLP_CHUNK), 1)
    causal = col <= row
    for g in range(CMLP_GROUPS):
        gl = g * LANES
        wg = jnp.where(causal, wsp_ref[g], 0.0).astype(BF16)
        n_chunks = tm // CMLP_CHUNK
        v_chunks = jnp.concatenate(
            [vb[c * CMLP_CHUNK:(c + 1) * CMLP_CHUNK, gl:gl + LANES] for c in range(n_chunks)], axis=1)
        s_chunks = jnp.dot(wg, v_chunks, preferred_element_type=F32)
        for c in range(n_chunks):
            cl = c * CMLP_CHUNK
            s = s_chunks[:, c * LANES:(c + 1) * LANES] + bias_ref[:, gl:gl + LANES]
            pre_ref[cl:cl + CMLP_CHUNK, gl:gl + LANES] = (u[cl:cl + CMLP_CHUNK, gl:gl + LANES] * s).astype(BF16)
    y = jnp.dot(pre_ref[...], wout_ref[...], preferred_element_type=F32)
    o_ref[...] = _layer_norm(DEEPNORM_ALPHA * x + y, g_ref[...], b_ref[...])


def _cmlp(h, w_in, ln_g, ln_b, w_sp, bias_full, w_out, g, b, *, emit_v, tm=512):
    n = h.shape[0]
    n_out = 2 if emit_v else 1
    return pl.pallas_call(
        _cmlp_kernel,
        grid=(n // tm,),
        in_specs=[
            pl.BlockSpec((tm, D_MODEL), lambda i: (i, 0)),
            _resident((D_MODEL, 2 * CMLP_DV)),
            _resident((1, CMLP_DV)),
            _resident((1, CMLP_DV)),
            _resident((CMLP_GROUPS, CMLP_CHUNK, CMLP_CHUNK)),
            _resident((CMLP_CHUNK, CMLP_DV)),
            _resident((CMLP_DV, D_MODEL)),
            _resident((1, D_MODEL)),
            _resident((1, D_MODEL)),
        ],
        out_specs=[pl.BlockSpec((tm, D_MODEL), lambda i: (i, 0))] * n_out,
        out_shape=[jax.ShapeDtypeStruct((n, D_MODEL), F32)] * n_out,
        scratch_shapes=[pltpu.VMEM((tm, CMLP_DV), BF16)],
        compiler_params=_params(("parallel",)),
        name="cmlp",
    )(h, w_in, ln_g, ln_b, w_sp, bias_full, w_out, g, b)


def _qkv_kernel(h_ref, w_ref, cos_ref, sin_ref, q_ref, k_ref, v_ref, *attn_refs):
    tm = h_ref.shape[0]
    qkv = jnp.dot(h_ref[...].astype(BF16), w_ref[...], preferred_element_type=F32)
    cos = cos_ref[...]
    sin = sin_ref[...]
    for hd in range(HEADS):
        lo = hd * HEAD_DIM
        qh = qkv[:, lo:lo + HEAD_DIM]
        q_ref[:, lo:lo + HEAD_DIM] = (qh * cos + pltpu.roll(qh, HEAD_DIM // 2, axis=1) * sin).astype(BF16)
        kh = qkv[:, D_MODEL + lo:D_MODEL + lo + HEAD_DIM]
        k_ref[:, lo:lo + HEAD_DIM] = kh * cos + pltpu.roll(kh, HEAD_DIM // 2, axis=1) * sin
    v = qkv[:, 2 * D_MODEL:]
    v_ref[...] = v
    if attn_refs:
        kb_ref, vt_ref, km_ref = attn_refs
        kb_ref[...] = k_ref[...].astype(BF16)
        vt_ref[...] = v.T.astype(BF16)
        for j in range(tm // MOBA_BLOCK):
            blk = k_ref[j * MOBA_BLOCK:(j + 1) * MOBA_BLOCK, :]
            km_ref[0, j:j + 1, :] = jnp.sum(blk, axis=0, keepdims=True) * (1.0 / MOBA_BLOCK)


def _qkv(h, w_qkv, cos, sin, *, for_prompt, tm=512):
    n = h.shape[0]
    nkb = tm // MOBA_BLOCK
    row_spec = lambda width: pl.BlockSpec((tm, width), lambda i: (i, 0))
    out_specs = [row_spec(D_MODEL)] * 3
    out_shape = [
        jax.ShapeDtypeStruct((n, D_MODEL), BF16),
        jax.ShapeDtypeStruct((n, D_MODEL), F32),
        jax.ShapeDtypeStruct((n, D_MODEL), F32),
    ]
    if for_prompt:
        out_specs += [
            row_spec(D_MODEL),
            pl.BlockSpec((D_MODEL, tm), lambda i: (0, i)),
            pl.BlockSpec((1, nkb, D_MODEL), lambda i: (i, 0, 0)),
        ]
        out_shape += [
            jax.ShapeDtypeStruct((n, D_MODEL), BF16),
            jax.ShapeDtypeStruct((D_MODEL, n), BF16),
            jax.ShapeDtypeStruct((n // tm, nkb, D_MODEL), F32),
        ]
    return pl.pallas_call(
        _qkv_kernel,
        grid=(n // tm,),
        in_specs=[
            row_spec(D_MODEL),
            _resident((D_MODEL, 3 * D_MODEL)),
            row_spec(HEAD_DIM),
            row_spec(HEAD_DIM),
        ],
        out_specs=out_specs,
        out_shape=out_shape,
        compiler_params=_params(("parallel",)),
        name="qkv_rope",
    )(h, w_qkv, cos, sin)


def _top3_sublanes(gate, n_valid):
    n_rows = gate.shape[0]
    blk = lax.broadcasted_iota(jnp.int32, gate.shape, 0)
    gate = jnp.where(blk < n_valid, gate, NEG_INF)
    blk = blk.astype(F32)
    bias = jnp.full(gate.shape, NEG_INF, F32)
    for _ in range(MOBA_TOPK):
        m = jnp.max(gate, axis=0, keepdims=True)
        first = jnp.min(jnp.where(gate == m, blk, float(n_rows)), axis=0, keepdims=True)
        pick = blk == first
        bias = jnp.where(pick & (m > NEG_INF), 0.0, bias)
        gate = jnp.where(pick, NEG_INF, gate)
    return bias


def _top3_lanes(gate, n_valid):
    lane = lax.broadcasted_iota(jnp.int32, gate.shape, 1)
    gate = jnp.where(lane < n_valid, gate, NEG_INF)
    bias = jnp.full(gate.shape, NEG_INF, F32)
    for _ in range(MOBA_TOPK):
        m = jnp.max(gate, axis=1, keepdims=True)
        first = jnp.min(jnp.where(gate == m, lane, LANES), axis=1, keepdims=True)
        pick = lane == first
        bias = jnp.where(pick & (m > NEG_INF), 0.0, bias)
        gate = jnp.where(pick, NEG_INF, gate)
    return bias


def _attn_prompt_kernel(q_ref, k_ref, vt_ref, km_ref, o_ref, bias_ref, acc_ref):
    i = pl.program_id(1)
    tq = q_ref.shape[0]
    c_exp = HEAD_DIM ** -0.5 * math.log2(math.e)
    nt = (((1,), (1,)), ((), ()))

    def scores(h, start):
        lo = h * HEAD_DIM
        kj = k_ref[0, pl.ds(start, MOBA_BLOCK), lo:lo + HEAD_DIM]
        return lax.dot_general(kj, q_ref[:, lo:lo + HEAD_DIM], nt, preferred_element_type=F32)

    def weighted_values(h, start, p):
        lo = h * HEAD_DIM
        return jnp.dot(vt_ref[lo:lo + HEAD_DIM, pl.ds(start, MOBA_BLOCK)], p.astype(BF16),
                       preferred_element_type=F32)

    own = pl.multiple_of(i * MOBA_BLOCK, MOBA_BLOCK)
    key = lax.broadcasted_iota(jnp.int32, (MOBA_BLOCK, tq), 0)
    qry = lax.broadcasted_iota(jnp.int32, (MOBA_BLOCK, tq), 1)
    gates = [
        lax.dot_general(km_ref[0, :, h * HEAD_DIM:(h + 1) * HEAD_DIM],
                        q_ref[:, h * HEAD_DIM:(h + 1) * HEAD_DIM].astype(F32), nt,
                        precision=HIGHEST, preferred_element_type=F32)
        for h in range(HEADS)
    ]
    own_scores = [scores(h, own) for h in range(HEADS)]
    ms, ls, ps = [], [], []
    for h in range(HEADS):
        bias_ref[h] = _top3_sublanes(gates[h], i)
        s = jnp.where(key <= qry, own_scores[h], NEG_INF)
        m = jnp.max(s, axis=0, keepdims=True)
        p = jnp.exp2((s - m) * c_exp)
        ms.append(m)
        ls.append(jnp.sum(p, axis=0, keepdims=True))
        ps.append(p)
    for h in range(HEADS):
        acc_ref[h] = weighted_values(h, own, ps[h])

    def body(j, carry):
        ms, ls = carry
        start = pl.multiple_of(j * MOBA_BLOCK, MOBA_BLOCK)
        ss = [scores(h, start) for h in range(HEADS)]
        new_ms, new_ls, alphas, ps = [], [], [], []
        for h in range(HEADS):
            s = ss[h] + bias_ref[h, pl.ds(j, 1), :]
            m_new = jnp.maximum(ms[h], jnp.max(s, axis=0, keepdims=True))
            alpha = jnp.exp2((ms[h] - m_new) * c_exp)
            p = jnp.exp2((s - m_new) * c_exp)
            new_ms.append(m_new)
            new_ls.append(alpha * ls[h] + jnp.sum(p, axis=0, keepdims=True))
            alphas.append(alpha)
            ps.append(p.astype(BF16))
        for h in range(HEADS):
            acc_ref[h] = alphas[h] * acc_ref[h] + weighted_values(h, start, ps[h])
        return tuple(new_ms), tuple(new_ls)

    _, ls = lax.fori_loop(0, i, body, (tuple(ms), tuple(ls)))
    for h in range(HEADS):
        o_ref[h * HEAD_DIM:(h + 1) * HEAD_DIM, :] = (acc_ref[h] / ls[h]).astype(BF16)


def _attn_prompt(q, k, vt, kmean):
    bn, t, _ = k.shape
    nb = t // MOBA_BLOCK
    tq = MOBA_BLOCK
    nq = t // tq
    return pl.pallas_call(
        _attn_prompt_kernel,
        grid=(bn, nq),
        in_specs=[
            pl.BlockSpec((tq, D_MODEL), lambda b, i: (b * nq + i, 0)),
            pl.BlockSpec((1, t, D_MODEL), lambda b, i: (b, 0, 0), pipeline_mode=pl.Buffered(1)),
            pl.BlockSpec((D_MODEL, t), lambda b, i: (0, b), pipeline_mode=pl.Buffered(1)),
            pl.BlockSpec((1, nb, D_MODEL), lambda b, i: (b, 0, 0)),
        ],
        out_specs=pl.BlockSpec((D_MODEL, tq), lambda b, i: (0, b * nq + i)),
        out_shape=jax.ShapeDtypeStruct((D_MODEL, bn * t), BF16),
        scratch_shapes=[
            pltpu.VMEM((HEADS, nb, tq), F32),
            pltpu.VMEM((HEADS, HEAD_DIM, tq), F32),
        ],
        compiler_params=_params(("parallel", "arbitrary")),
        name="moba_prompt",
    )(q, k, vt, kmean)


def _attn_sample_kernel(pt_ref, q_ref, kn_ref, vn_ref, *rest, n_pages):
    kp_refs = rest[:n_pages]
    vp_refs = rest[n_pages:2 * n_pages]
    o_ref, kb_ref, vb_ref, s_ref = rest[2 * n_pages:]
    del pt_ref
    tq = q_ref.shape[1]
    past = n_pages * PAGE_SIZE
    n_past_blocks = past // MOBA_BLOCK
    n_new = 2 * SUBLANES
    c_exp = HEAD_DIM ** -0.5 * math.log2(math.e)

    q = q_ref[0].astype(F32)
    nr = HEADS * tq
    qt = jnp.concatenate([q] * HEADS, axis=0)
    rid = lax.broadcasted_iota(jnp.int32, (nr, D_MODEL), 0)
    lid = lax.broadcasted_iota(jnp.int32, (nr, D_MODEL), 1)
    qr = jnp.where(lid // HEAD_DIM == rid // tq, qt, 0.0).astype(BF16)

    nt = (((1,), (1,)), ((), ()))
    for p in range(n_pages):
        rows = slice(p * PAGE_SIZE, (p + 1) * PAGE_SIZE)
        for hd in range(HEADS):
            lo = hd * HEAD_DIM
            head_rows = pl.ds(hd, PAGE_SIZE, stride=HEADS)
            kb_ref[rows, lo:lo + HEAD_DIM] = kp_refs[p][0, head_rows, :].astype(BF16)
            vb_ref[rows, lo:lo + HEAD_DIM] = vp_refs[p][0, head_rows, :].astype(BF16)
        s_ref[:, rows] = lax.dot_general(qr, kb_ref[rows, :], nt, preferred_element_type=F32)
    zpad = jnp.zeros((n_new - tq, D_MODEL), F32)
    kb_ref[past:past + n_new, :] = jnp.concatenate([kn_ref[0], zpad], axis=0).astype(BF16)
    vb_ref[past:past + n_new, :] = jnp.concatenate([vn_ref[0], zpad], axis=0).astype(BF16)

    lane = lax.broadcasted_iota(jnp.int32, (nr, LANES), 1)
    gate = jnp.full((nr, LANES), NEG_INF, F32)
    for n in range(n_past_blocks):
        lo = n * MOBA_BLOCK
        block_mean = jnp.sum(s_ref[:, lo:lo + MOBA_BLOCK], axis=1, keepdims=True) * (1.0 / MOBA_BLOCK)
        gate = jnp.where(lane == n, block_mean, gate)
    bias = _top3_lanes(gate, n_past_blocks)

    s_new = lax.dot_general(qr, kb_ref[past:past + n_new, :], nt, preferred_element_type=F32)
    qi = lax.broadcasted_iota(jnp.int32, (nr, n_new), 0) % tq
    kr = lax.broadcasted_iota(jnp.int32, (nr, n_new), 1)
    s_new = jnp.where(kr <= qi, s_new, NEG_INF)
    m = jnp.max(s_new, axis=1, keepdims=True)
    for n in range(n_past_blocks):
        lo = n * MOBA_BLOCK
        sb = s_ref[:, lo:lo + MOBA_BLOCK] + bias[:, n:n + 1]
        s_ref[:, lo:lo + MOBA_BLOCK] = sb
        m = jnp.maximum(m, jnp.max(sb, axis=1, keepdims=True))
    p_new = jnp.exp2((s_new - m) * c_exp)
    l = jnp.sum(p_new, axis=1, keepdims=True)
    for n in range(n_past_blocks):
        lo = n * MOBA_BLOCK
        pb = jnp.exp2((s_ref[:, lo:lo + MOBA_BLOCK] - m) * c_exp)
        s_ref[:, lo:lo + MOBA_BLOCK] = pb
        l = l + jnp.sum(pb, axis=1, keepdims=True)
    inv_l = 1.0 / l
    out = jnp.dot((s_ref[:, :past] * inv_l).astype(BF16), vb_ref[:past, :], preferred_element_type=F32)
    out = out + jnp.dot((p_new * inv_l).astype(BF16), vb_ref[past:past + n_new, :], preferred_element_type=F32)
    for hd in range(HEADS):
        lo = hd * HEAD_DIM
        o_ref[0, :, lo:lo + HEAD_DIM] = out[hd * tq:(hd + 1) * tq, lo:lo + HEAD_DIM].astype(BF16)


def _attn_sample(q, k_new, v_new, cache_k, cache_v, page_table):
    ns, tq, _ = q.shape
    n_pages = page_table.shape[1]
    past = n_pages * PAGE_SIZE
    n_new = 2 * SUBLANES
    assert tq == SUBLANES and HEADS * tq <= LANES and past % MOBA_BLOCK == 0

    def page_spec(p):
        return pl.BlockSpec((1, PAGE_SIZE * HEADS, HEAD_DIM), lambda s, pt: (pt[s * n_pages + p], 0, 0))

    seq_spec = pl.BlockSpec((1, tq, D_MODEL), lambda s, pt: (s, 0, 0))
    grid_spec = pltpu.PrefetchScalarGridSpec(
        num_scalar_prefetch=1,
        grid=(ns,),
        in_specs=[seq_spec, seq_spec, seq_spec]
        + [page_spec(p) for p in range(n_pages)]
        + [page_spec(p) for p in range(n_pages)],
        out_specs=seq_spec,
        scratch_shapes=[
            pltpu.VMEM((past + n_new, D_MODEL), BF16),
            pltpu.VMEM((past + n_new, D_MODEL), BF16),
            pltpu.VMEM((HEADS * tq, past), F32),
        ],
    )
    return pl.pallas_call(
        functools.partial(_attn_sample_kernel, n_pages=n_pages),
        grid_spec=grid_spec,
        out_shape=jax.ShapeDtypeStruct((ns, tq, D_MODEL), BF16),
        compiler_params=_params(("arbitrary",)),
        name="moba_sample",
    )(page_table.reshape(-1), q, k_new, v_new, *([cache_k] * n_pages), *([cache_v] * n_pages))


def _ssd_chunk(raw, z, dtr, first_chunk, conv0_ref, ssm0_ref, wconv_ref, bconv_ref, dtb_ref, alog_ref,
               dskip_ref, ng_ref, tri_ref, y_ref, convo_ref, state_ref, ext_ref, yacc_ref):
    t = raw.shape[0]
    L = tri_ref.shape[0]
    halo = SUBLANES

    @pl.when(first_chunk)
    def _():
        ext_ref[0:halo, :] = conv0_ref[...]
        state_ref[...] = ssm0_ref[...]

    ext_ref[halo:halo + t, :] = raw
    if t < L:
        ext_ref[halo + t:halo + L, :] = jnp.zeros((L - t, SSM_CONV_DIM), F32)
    ext = ext_ref[...]
    conv = bconv_ref[...] + wconv_ref[SSM_CONV - 1:SSM_CONV, :] * ext[halo:, :]
    for back in range(1, SSM_CONV):
        w = SSM_CONV - 1 - back
        conv = conv + wconv_ref[w:w + 1, :] * pltpu.roll(ext, back, axis=0)[halo:, :]
    tail = ext_ref[t:t + halo, :]
    ext_ref[0:halo, :] = tail
    convo_ref[...] = tail
    xbc = _silu(conv)
    x = xbc[:, :SSM_D_INNER]
    bm = xbc[:, SSM_D_INNER:SSM_D_INNER + SSM_BC_DIM].astype(BF16)
    cm = xbc[:, SSM_D_INNER + SSM_BC_DIM:]

    dtx = dtr + dtb_ref[...]
    dt = jnp.maximum(dtx, 0.0) + jnp.log(1.0 + jnp.exp(-jnp.abs(dtx)))
    if t < L:
        dt = jnp.concatenate([dt, jnp.zeros((L - t, LANES), F32)], axis=0)
    a = -jnp.exp(alog_ref[...])
    adt = dt * a
    acs = jnp.dot(tri_ref[...], adt, precision=HIGHEST, preferred_element_type=F32)
    acs_t = acs.T
    dt_t = dt.T
    w_t = jnp.exp(acs_t[:, L - 1:L] - acs_t) * dt_t
    x_t = x.T
    xb = x.astype(BF16)

    row = lax.broadcasted_iota(jnp.int32, (L, L), 0)
    col = lax.broadcasted_iota(jnp.int32, (L, L), 1)
    causal = col <= row
    nt = (((1,), (1,)), ((), ()))
    for g in range(SSM_GROUPS):
        bg = bm[:, g * SSM_D_STATE:(g + 1) * SSM_D_STATE]
        cg = cm[:, g * SSM_D_STATE:(g + 1) * SSM_D_STATE]
        cb = lax.dot_general(cg.astype(BF16), bg, nt, preferred_element_type=F32)
        for e in range(SSM_HPG):
            hd = g * SSM_HPG + e
            lo = hd * SSM_HEAD_DIM
            acs_col = jnp.broadcast_to(acs[:, hd:hd + 1], (L, SSM_D_STATE))
            lmat = jnp.where(causal, jnp.exp(acs_col[:, :L] - acs_t[hd:hd + 1, :]), 0.0)
            mh = (cb * lmat * dt_t[hd:hd + 1, :]).astype(BF16)
            ch = (cg * jnp.exp(acs_col)).astype(BF16)
            sh = state_ref[lo:lo + SSM_HEAD_DIM, :]
            yh = jnp.dot(mh, xb[:, lo:lo + SSM_HEAD_DIM], preferred_element_type=F32)
            yh = yh + lax.dot_general(ch, sh.astype(BF16), nt, preferred_element_type=F32)
            yacc_ref[:, lo:lo + SSM_HEAD_DIM] = yh
            xw = (x_t[lo:lo + SSM_HEAD_DIM, :] * w_t[hd:hd + 1, :]).astype(BF16)
            upd = jnp.dot(xw, bg, preferred_element_type=F32)
            state_ref[lo:lo + SSM_HEAD_DIM, :] = jnp.exp(acs_t[hd:hd + 1, L - 1:L]) * sh + upd

    y = yacc_ref[0:t, :] + x[:t] * dskip_ref[...]
    y = y * _silu(z)
    gw = SSM_D_INNER // SSM_GROUPS
    for g in range(SSM_GROUPS):
        yg = y[:, g * gw:(g + 1) * gw]
        yg = yg * lax.rsqrt(jnp.mean(yg * yg, axis=-1, keepdims=True) + RMS_EPS)
        y_ref[:, g * gw:(g + 1) * gw] = (yg * ng_ref[:, g * gw:(g + 1) * gw]).astype(y_ref.dtype)


def _ssd_kernel(h_ref, conv0_ref, ssm0_ref, wz_ref, wx_ref, wdt_ref, *rest):
    *params, y_ref, convo_ref, ssmo_ref, ext_ref, yacc_ref = rest
    seqs, t, _ = h_ref.shape
    first_chunk = pl.program_id(1) == 0
    xb = h_ref[...].reshape(seqs * t, D_MODEL).astype(BF16)
    z = jnp.dot(xb, wz_ref[...], preferred_element_type=F32)
    raw = jnp.dot(xb, wx_ref[...], preferred_element_type=F32)
    dtr = jnp.dot(xb, wdt_ref[...], preferred_element_type=F32)
    for s in range(seqs):
        rows = slice(s * t, (s + 1) * t)
        _ssd_chunk(raw[rows], z[rows], dtr[rows], first_chunk, conv0_ref.at[s], ssm0_ref.at[s], *params,
                   y_ref.at[s], convo_ref.at[s], ssmo_ref.at[s], ext_ref.at[s], yacc_ref.at[s])


def _ssd(h, conv0, ssm0, w_z, w_x, w_dt, w_conv, b_conv, dt_bias, a_log, d_skip, norm_g, *, seqs_per_step):
    bn, t_total, _ = h.shape
    t = min(t_total, SSM_CHUNK)
    nc = t_total // t
    chunk = max(t, 2 * SUBLANES)
    tri = jnp.tril(jnp.ones((chunk, chunk), F32))
    sq = seqs_per_step
    blk = lambda width: pl.BlockSpec((sq, t, width), lambda b, c: (b, c, 0))
    per_seq = lambda rows, width: pl.BlockSpec((sq, rows, width), lambda b, c: (b, 0, 0))
    return pl.pallas_call(
        _ssd_kernel,
        grid=(bn // sq, nc),
        in_specs=[
            blk(D_MODEL),
            per_seq(SUBLANES, SSM_CONV_DIM),
            per_seq(SSM_D_INNER, SSM_D_STATE),
            _resident((D_MODEL, SSM_D_INNER)),
            _resident((D_MODEL, SSM_CONV_DIM)),
            _resident((D_MODEL, LANES)),
            _resident((SUBLANES, SSM_CONV_DIM)),
            _resident((1, SSM_CONV_DIM)),
            _resident((1, LANES)),
            _resident((1, LANES)),
            _resident((1, SSM_D_INNER)),
            _resident((1, SSM_D_INNER)),
            _resident((chunk, chunk)),
        ],
        out_specs=[
            blk(SSM_D_INNER),
            per_seq(SUBLANES, SSM_CONV_DIM),
            per_seq(SSM_D_INNER, SSM_D_STATE),
        ],
        out_shape=[
            jax.ShapeDtypeStruct((bn, t_total, SSM_D_INNER), BF16),
            jax.ShapeDtypeStruct((bn, SUBLANES, SSM_CONV_DIM), F32),
            jax.ShapeDtypeStruct((bn, SSM_D_INNER, SSM_D_STATE), F32),
        ],
        scratch_shapes=[
            pltpu.VMEM((sq, SUBLANES + chunk, SSM_CONV_DIM), F32),
            pltpu.VMEM((sq, chunk, SSM_D_INNER), F32),
        ],
        compiler_params=_params(("arbitrary", "arbitrary")),
        name="ssd",
    )(h, conv0, ssm0, w_z, w_x, w_dt, w_conv, b_conv, dt_bias, a_log, d_skip, norm_g, tri)


def _rope_tables(pos):
    half = HEAD_DIM // 2
    inv = ROPE_THETA ** (-jnp.arange(half, dtype=F32) / half)
    ang = pos.astype(F32)[:, None] * inv[None, :]
    cos, sin = jnp.cos(ang), jnp.sin(ang)
    return jnp.concatenate([cos, cos], axis=-1), jnp.concatenate([-sin, sin], axis=-1)


def _row(v):
    return v.reshape(1, -1)


def kernel(x_prompt, x_sample, cache_k, cache_v, state_conv, state_ssm, page_table, ln_g, ln_b, ffn_w_in, ffn_w_out, cmlp_w_in, cmlp_ln_g, cmlp_ln_b, cmlp_w_s, cmlp_b_s, cmlp_w_out, moba_w_qkv, moba_w_out, ssm_w_in, ssm_w_conv, ssm_b_conv, ssm_dt_bias, ssm_a_log, ssm_d, ssm_norm_g, ssm_w_out):
    n_prompt, t_prompt, _ = x_prompt.shape
    n_sample, t_sample, _ = x_sample.shape
    n_pages = page_table.shape[1]
    past_len = n_pages * PAGE_SIZE
    hp = x_prompt.reshape(n_prompt * t_prompt, D_MODEL)
    hs = x_sample.reshape(n_sample * t_sample, D_MODEL)

    cos_p, sin_p = _rope_tables(jnp.arange(t_prompt))
    cos_p, sin_p = jnp.tile(cos_p, (n_prompt, 1)), jnp.tile(sin_p, (n_prompt, 1))
    cos_s, sin_s = _rope_tables(past_len + jnp.arange(t_sample))
    cos_s, sin_s = jnp.tile(cos_s, (n_sample, 1)), jnp.tile(sin_s, (n_sample, 1))

    outs = {k: [] for k in ("cmlp_v", "k_p", "v_p", "k_s", "v_s", "conv_p", "ssm_p", "conv_s", "ssm_s")}
    for i in range(DEPTH):
        w_in, w_out = ffn_w_in[i, 0].astype(BF16), ffn_w_out[i, 0].astype(BF16)
        hp, hs = _ffn(hp, hs, w_in, w_out, _row(ln_g[i, 0]), _row(ln_b[i, 0]))
        kind, j = i % 3, i // 3
        g1, b1 = _row(ln_g[i, 1]), _row(ln_b[i, 1])
        if kind == 0:
            samples_per_chunk = CMLP_CHUNK // t_sample
            w_sp_s = jnp.einsum("ab,gts->gatbs", jnp.eye(samples_per_chunk, dtype=F32),
                                cmlp_w_s[j][:, :t_sample, :t_sample]).reshape(CMLP_GROUPS, CMLP_CHUNK, CMLP_CHUNK)
            bias_p = jnp.repeat(cmlp_b_s[j].T, CMLP_DV // CMLP_GROUPS, axis=1)
            bias_s = jnp.tile(jnp.repeat(cmlp_b_s[j][:, :t_sample].T, CMLP_DV // CMLP_GROUPS, axis=1),
                              (samples_per_chunk, 1))
            a_par = (cmlp_w_in[j].astype(BF16), _row(cmlp_ln_g[j]), _row(cmlp_ln_b[j]))
            w_o = cmlp_w_out[j].astype(BF16)
            (hp,) = _cmlp(hp, *a_par, cmlp_w_s[j], bias_p, w_o, g1, b1, emit_v=False)
            hs, v_rows = _cmlp(hs, *a_par, w_sp_s, bias_s, w_o, g1, b1, emit_v=True)
            outs["cmlp_v"].append(v_rows.reshape(n_sample, t_sample, CMLP_DV))
        elif kind == 1:
            w_qkv = moba_w_qkv[j].astype(BF16)
            w_o = moba_w_out[j].astype(BF16)
            q, k, v, kb, vt, kmean = _qkv(hp, w_qkv, cos_p, sin_p, for_prompt=True)
            o_t = _attn_prompt(q, kb.reshape(n_prompt, t_prompt, D_MODEL), vt,
                               kmean.reshape(n_prompt, t_prompt // MOBA_BLOCK, D_MODEL))
            hp = _proj_ln(hp, o_t, w_o, g1, b1, pre_transposed=True)
            outs["k_p"].append(k.reshape(n_prompt, t_prompt, HEADS, HEAD_DIM))
            outs["v_p"].append(v.reshape(n_prompt, t_prompt, HEADS, HEAD_DIM))
            q, k, v = _qkv(hs, w_qkv, cos_s, sin_s, for_prompt=False)
            shs = (n_sample, t_sample, D_MODEL)
            pool = cache_k.shape[1]
            o = _attn_sample(q.reshape(shs), k.reshape(shs), v.reshape(shs),
                             cache_k.reshape(-1, PAGE_SIZE * HEADS, HEAD_DIM),
                             cache_v.reshape(-1, PAGE_SIZE * HEADS, HEAD_DIM), page_table + j * pool)
            hs = _proj_ln(hs, o.reshape(-1, D_MODEL), w_o, g1, b1)
            outs["k_s"].append(k.reshape(n_sample, t_sample, HEADS, HEAD_DIM))
            outs["v_s"].append(v.reshape(n_sample, t_sample, HEADS, HEAD_DIM))
        else:
            w = ssm_w_in[j]
            w_z = w[:, :SSM_D_INNER].astype(BF16)
            w_x = w[:, SSM_D_INNER:SSM_D_INNER + SSM_CONV_DIM].astype(BF16)
            w_dt = jnp.pad(w[:, SSM_D_INNER + SSM_CONV_DIM:], ((0, 0), (0, LANES - SSM_HEADS))).astype(BF16)
            c_par = (
                w_z, w_x, w_dt,
                jnp.pad(ssm_w_conv[j], ((0, SUBLANES - SSM_CONV), (0, 0))),
                _row(ssm_b_conv[j]),
                _row(jnp.pad(ssm_dt_bias[j], (0, LANES - SSM_HEADS))),
                _row(jnp.pad(ssm_a_log[j], (0, LANES - SSM_HEADS))),
                _row(jnp.repeat(ssm_d[j], SSM_HEAD_DIM)),
                _row(ssm_norm_g[j]),
            )
            w_o = ssm_w_out[j].astype(BF16)
            halo_pad = ((0, 0), (SUBLANES - (SSM_CONV - 1), 0), (0, 0))

            y, conv_t, ssm_f = _ssd(
                hp.reshape(n_prompt, t_prompt, D_MODEL),
                jnp.zeros((n_prompt, SUBLANES, SSM_CONV_DIM), F32),
                jnp.zeros((n_prompt, SSM_D_INNER, SSM_D_STATE), F32), *c_par, seqs_per_step=1)
            hp = _proj_ln(hp, y.reshape(-1, SSM_D_INNER), w_o, g1, b1)
            outs["conv_p"].append(conv_t[:, SUBLANES - (SSM_CONV - 1):])
            outs["ssm_p"].append(ssm_f.reshape(n_prompt, SSM_HEADS, SSM_HEAD_DIM, SSM_D_STATE))

            y, conv_t, ssm_f = _ssd(
                hs.reshape(n_sample, t_sample, D_MODEL),
                jnp.pad(state_conv[j], halo_pad),
                state_ssm[j].reshape(n_sample, SSM_D_INNER, SSM_D_STATE), *c_par,
                seqs_per_step=SSD_SAMPLE_SEQS)
            hs = _proj_ln(hs, y.reshape(-1, SSM_D_INNER), w_o, g1, b1)
            outs["conv_s"].append(conv_t[:, SUBLANES - (SSM_CONV - 1):])
            outs["ssm_s"].append(ssm_f.reshape(n_sample, SSM_HEADS, SSM_HEAD_DIM, SSM_D_STATE))
        w_in, w_out = ffn_w_in[i, 1].astype(BF16), ffn_w_out[i, 1].astype(BF16)
        hp, hs = _ffn(hp, hs, w_in, w_out, _row(ln_g[i, 2]), _row(ln_b[i, 2]))
    st = lambda name: jnp.stack(outs[name])
    return (hp.reshape(n_prompt, t_prompt, D_MODEL), hs.reshape(n_sample, t_sample, D_MODEL),
            st("cmlp_v"), st("k_p"), st("v_p"), st("k_s"), st("v_s"),
            st("conv_p"), st("ssm_p"), st("conv_s"), st("ssm_s"))
```

```python
import functools
import math

import jax
import jax.numpy as jnp
from jax import lax
from jax.experimental import pallas as pl
from jax.experimental.pallas import tpu as pltpu

F32 = jnp.float32
BF16 = jnp.bfloat16
HIGHEST = lax.Precision.HIGHEST

LANES = 128
SUBLANES = 8
VMEM_LIMIT_BYTES = 56 * 1024 * 1024

DEPTH = 4
D_MODEL = 1024
D_FF = 2816
DEEPNORM_ALPHA = (2 * DEPTH) ** 0.25
LN_EPS = 1e-5
RMS_EPS = 1e-5
FFN_HALF = 0.5

CMLP_CHUNK = 128
CMLP_GROUPS = 8
CMLP_DV = D_MODEL

HEAD_DIM = 128
HEADS = D_MODEL // HEAD_DIM
MOBA_BLOCK = 256
MOBA_TOPK = 3
MOBA_QUERY_TILE = 2 * MOBA_BLOCK
PAGE_SIZE = 128
ROPE_THETA = 10000.0

SSM_D_INNER = 2 * D_MODEL
SSM_HEAD_DIM = 64
SSM_HEADS = SSM_D_INNER // SSM_HEAD_DIM
SSM_GROUPS = 8
SSM_HPG = SSM_HEADS // SSM_GROUPS
SSM_D_STATE = 128
SSM_CONV = 4
SSM_BC_DIM = SSM_GROUPS * SSM_D_STATE
SSM_CONV_DIM = SSM_D_INNER + 2 * SSM_BC_DIM
SSM_CHUNK = 128
SSD_SAMPLE_SEQS = 4

NEG_INF = float("-inf")


def _params(semantics):
    return pltpu.CompilerParams(dimension_semantics=semantics, vmem_limit_bytes=VMEM_LIMIT_BYTES)


def _resident(shape):
    zeros = (0,) * len(shape)
    return pl.BlockSpec(shape, lambda *_: zeros, pipeline_mode=pl.Buffered(1))


def _layer_norm(y, g, b):
    mu = jnp.mean(y, axis=-1, keepdims=True)
    d = y - mu
    var = jnp.mean(d * d, axis=-1, keepdims=True)
    return d * lax.rsqrt(var + LN_EPS) * g + b


def _silu(x):
    return x * jax.nn.sigmoid(x)


FFN_COL_CHUNK = 256


def _ffn_tile(x_ref, win_ref, wout_ref, g_ref, b_ref, o_ref, act_ref):
    x = x_ref[...]
    xb = x.astype(BF16)
    for c in range(D_FF // FFN_COL_CHUNK):
        lo = c * FFN_COL_CHUNK
        gate = jnp.dot(xb, win_ref[:, lo:lo + FFN_COL_CHUNK], preferred_element_type=F32)
        up = jnp.dot(xb, win_ref[:, D_FF + lo:D_FF + lo + FFN_COL_CHUNK], preferred_element_type=F32)
        act_ref[:, lo:lo + FFN_COL_CHUNK] = (_silu(gate) * up).astype(BF16)
    y = jnp.dot(act_ref[...], wout_ref[...], preferred_element_type=F32)
    o_ref[...] = _layer_norm(DEEPNORM_ALPHA * x + FFN_HALF * y, g_ref[...], b_ref[...])


def _ffn_kernel(xp_ref, xs_ref, win_ref, wout_ref, g_ref, b_ref, op_ref, os_ref, act_ref, *, n_prompt_tiles):
    i = pl.program_id(0)

    @pl.when(i < n_prompt_tiles)
    def _():
        _ffn_tile(xp_ref, win_ref, wout_ref, g_ref, b_ref, op_ref, act_ref)

    @pl.when(i >= n_prompt_tiles)
    def _():
        _ffn_tile(xs_ref, win_ref, wout_ref, g_ref, b_ref, os_ref, act_ref)


def _ffn(hp, hs, w_in, w_out, g, b, *, tm=512):
    npt, nst = hp.shape[0] // tm, hs.shape[0] // tm
    prompt_spec = pl.BlockSpec((tm, D_MODEL), lambda i: (jnp.minimum(i, npt - 1), 0))
    sample_spec = pl.BlockSpec((tm, D_MODEL), lambda i: (jnp.maximum(i - npt, 0), 0))
    return pl.pallas_call(
        functools.partial(_ffn_kernel, n_prompt_tiles=npt),
        grid=(npt + nst,),
        in_specs=[
            prompt_spec,
            sample_spec,
            _resident((D_MODEL, 2 * D_FF)),
            _resident((D_FF, D_MODEL)),
            _resident((1, D_MODEL)),
            _resident((1, D_MODEL)),
        ],
        out_specs=[prompt_spec, sample_spec],
        out_shape=[jax.ShapeDtypeStruct(hp.shape, F32), jax.ShapeDtypeStruct(hs.shape, F32)],
        scratch_shapes=[pltpu.VMEM((tm, D_FF), BF16)],
        compiler_params=_params(("arbitrary",)),
        name="ffn",
    )(hp, hs, w_in, w_out, g, b)


def _proj_ln_kernel(h_ref, pre_ref, w_ref, g_ref, b_ref, o_ref, *, pre_transposed):
    contract_pre = 0 if pre_transposed else 1
    y = lax.dot_general(pre_ref[...], w_ref[...], (((contract_pre,), (0,)), ((), ())),
                        preferred_element_type=F32)
    o_ref[...] = _layer_norm(DEEPNORM_ALPHA * h_ref[...] + y, g_ref[...], b_ref[...])


def _proj_ln(h, pre, w_out, g, b, *, pre_transposed=False, tm=512):
    n = h.shape[0]
    k = w_out.shape[0]
    pre_spec = (pl.BlockSpec((k, tm), lambda i: (0, i)) if pre_transposed
                else pl.BlockSpec((tm, k), lambda i: (i, 0)))
    return pl.pallas_call(
        functools.partial(_proj_ln_kernel, pre_transposed=pre_transposed),
        grid=(n // tm,),
        in_specs=[
            pl.BlockSpec((tm, D_MODEL), lambda i: (i, 0)),
            pre_spec,
            _resident((k, D_MODEL)),
            _resident((1, D_MODEL)),
            _resident((1, D_MODEL)),
        ],
        out_specs=pl.BlockSpec((tm, D_MODEL), lambda i: (i, 0)),
        out_shape=jax.ShapeDtypeStruct((n, D_MODEL), F32),
        compiler_params=_params(("parallel",)),
        name="proj_ln",
    )(h, pre, w_out, g, b)


def _cmlp_kernel(h_ref, win_ref, lng_ref, lnb_ref, wsp_ref, bias_ref, wout_ref, g_ref, b_ref,
                 o_ref, *rest):
    *v_refs, pre_ref = rest
    tm = h_ref.shape[0]
    x = h_ref[...]
    uv = jnp.dot(x.astype(BF16), win_ref[...], preferred_element_type=F32)
    uv = 0.5 * uv * (1.0 + lax.erf(uv * math.sqrt(0.5)))
    u = uv[:, :CMLP_DV]
    v = _layer_norm(uv[:, CMLP_DV:], lng_ref[...], lnb_ref[...])
    for v_ref in v_refs:
        v_ref[...] = v
    vb = v.astype(BF16)
    row = lax.broadcasted_iota(jnp.int32, (CMLP_CHUNK, CMLP_CHUNK), 0)
    col = lax.broadcasted_iota(jnp.int32, (CMLP_CHUNK, CMLP_CHUNK), 1)
    causal = col <= row
    for g in range(CMLP_GROUPS):
        gl = g * LANES
        wg = jnp.where(causal, wsp_ref[g], 0.0).astype(BF16)
        n_chunks = tm // CMLP_CHUNK
        v_chunks = jnp.concatenate(
            [vb[c * CMLP_CHUNK:(c + 1) * CMLP_CHUNK, gl:gl + LANES] for c in range(n_chunks)], axis=1)
        s_chunks = jnp.dot(wg, v_chunks, preferred_element_type=F32)
        for c in range(n_chunks):
            cl = c * CMLP_CHUNK
            s = s_chunks[:, c * LANES:(c + 1) * LANES] + bias_ref[:, gl:gl + LANES]
            pre_ref[cl:cl + CMLP_CHUNK, gl:gl + LANES] = (u[cl:cl + CMLP_CHUNK, gl:gl + LANES] * s).astype(BF16)
    y = jnp.dot(pre_ref[...], wout_ref[...], preferred_element_type=F32)
    o_ref[...] = _layer_norm(DEEPNORM_ALPHA * x + y, g_ref[...], b_ref[...])


def _cmlp(h, w_in, ln_g, ln_b, w_sp, bias_full, w_out, g, b, *, emit_v, tm=512):
    n = h.shape[0]
    n_out = 2 if emit_v else 1
    return pl.pallas_call(
        _cmlp_kernel,
        grid=(n // tm,),
        in_specs=[
            pl.BlockSpec((tm, D_MODEL), lambda i: (i, 0)),
            _resident((D_MODEL, 2 * CMLP_DV)),
            _resident((1, CMLP_DV)),
            _resident((1, CMLP_DV)),
            _resident((CMLP_GROUPS, CMLP_CHUNK, CMLP_CHUNK)),
            _resident((CMLP_CHUNK, CMLP_DV)),
            _resident((CMLP_DV, D_MODEL)),
            _resident((1, D_MODEL)),
            _resident((1, D_MODEL)),
        ],
        out_specs=[pl.BlockSpec((tm, D_MODEL), lambda i: (i, 0))] * n_out,
        out_shape=[jax.ShapeDtypeStruct((n, D_MODEL), F32)] * n_out,
        scratch_shapes=[pltpu.VMEM((tm, CMLP_DV), BF16)],
        compiler_params=_params(("parallel",)),
        name="cmlp",
    )(h, w_in, ln_g, ln_b, w_sp, bias_full, w_out, g, b)


def _qkv_kernel(h_ref, w_ref, cos_ref, sin_ref, q_ref, k_ref, v_ref, *attn_refs):
    tm = h_ref.shape[0]
    qkv = jnp.dot(h_ref[...].astype(BF16), w_ref[...], preferred_element_type=F32)
    cos = cos_ref[...]
    sin = sin_ref[...]
    for hd in range(HEADS):
        lo = hd * HEAD_DIM
        qh = qkv[:, lo:lo + HEAD_DIM]
        q_ref[:, lo:lo + HEAD_DIM] = (qh * cos + pltpu.roll(qh, HEAD_DIM // 2, axis=1) * sin).astype(BF16)
        kh = qkv[:, D_MODEL + lo:D_MODEL + lo + HEAD_DIM]
        k_ref[:, lo:lo + HEAD_DIM] = kh * cos + pltpu.roll(kh, HEAD_DIM // 2, axis=1) * sin
    v = qkv[:, 2 * D_MODEL:]
    v_ref[...] = v
    if attn_refs:
        kb_ref, vt_ref, km_ref = attn_refs
        kb_ref[...] = k_ref[...].astype(BF16)
        vt_ref[...] = v.T.astype(BF16)
        for j in range(tm // MOBA_BLOCK):
            blk = k_ref[j * MOBA_BLOCK:(j + 1) * MOBA_BLOCK, :]
            km_ref[0, j:j + 1, :] = jnp.sum(blk, axis=0, keepdims=True) * (1.0 / MOBA_BLOCK)


def _qkv(h, w_qkv, cos, sin, *, for_prompt, tm=512):
    n = h.shape[0]
    nkb = tm // MOBA_BLOCK
    row_spec = lambda width: pl.BlockSpec((tm, width), lambda i: (i, 0))
    out_specs = [row_spec(D_MODEL)] * 3
    out_shape = [
        jax.ShapeDtypeStruct((n, D_MODEL), BF16),
        jax.ShapeDtypeStruct((n, D_MODEL), F32),
        jax.ShapeDtypeStruct((n, D_MODEL), F32),
    ]
    if for_prompt:
        out_specs += [
            row_spec(D_MODEL),
            pl.BlockSpec((D_MODEL, tm), lambda i: (0, i)),
            pl.BlockSpec((1, nkb, D_MODEL), lambda i: (i, 0, 0)),
        ]
        out_shape += [
            jax.ShapeDtypeStruct((n, D_MODEL), BF16),
            jax.ShapeDtypeStruct((D_MODEL, n), BF16),
            jax.ShapeDtypeStruct((n // tm, nkb, D_MODEL), F32),
        ]
    return pl.pallas_call(
        _qkv_kernel,
        grid=(n // tm,),
        in_specs=[
            row_spec(D_MODEL),
            _resident((D_MODEL, 3 * D_MODEL)),
            row_spec(HEAD_DIM),
            row_spec(HEAD_DIM),
        ],
        out_specs=out_specs,
        out_shape=out_shape,
        compiler_params=_params(("parallel",)),
        name="qkv_rope",
    )(h, w_qkv, cos, sin)


def _top3_sublanes(gate, n_valid):
    n_rows = gate.shape[0]
    blk = lax.broadcasted_iota(jnp.int32, gate.shape, 0)
    gate = jnp.where(blk < n_valid, gate, NEG_INF)
    blk = blk.astype(F32)
    bias = jnp.full(gate.shape, NEG_INF, F32)
    for _ in range(MOBA_TOPK):
        m = jnp.max(gate, axis=0, keepdims=True)
        first = jnp.min(jnp.where(gate == m, blk, float(n_rows)), axis=0, keepdims=True)
        pick = blk == first
        bias = jnp.where(pick & (m > NEG_INF), 0.0, bias)
        gate = jnp.where(pick, NEG_INF, gate)
    return bias


def _top3_lanes(gate, n_valid):
    lane = lax.broadcasted_iota(jnp.int32, gate.shape, 1)
    gate = jnp.where(lane < n_valid, gate, NEG_INF)
    bias = jnp.full(gate.shape, NEG_INF, F32)
    for _ in range(MOBA_TOPK):
        m = jnp.max(gate, axis=1, keepdims=True)
        first = jnp.min(jnp.where(gate == m, lane, LANES), axis=1, keepdims=True)
        pick = lane == first
        bias = jnp.where(pick & (m > NEG_INF), 0.0, bias)
        gate = jnp.where(pick, NEG_INF, gate)
    return bias


def _attn_prompt_kernel(q_ref, k_ref, vt_ref, km_ref, o_ref, bias_ref, acc_ref):
    i = pl.program_id(1)
    tq = q_ref.shape[0]
    c_exp = HEAD_DIM ** -0.5 * math.log2(math.e)
    nt = (((1,), (1,)), ((), ()))

    def scores(h, start):
        lo = h * HEAD_DIM
        kj = k_ref[0, pl.ds(start, MOBA_BLOCK), lo:lo + HEAD_DIM]
        return lax.dot_general(kj, q_ref[:, lo:lo + HEAD_DIM], nt, preferred_element_type=F32)

    def weighted_values(h, start, p):
        lo = h * HEAD_DIM
        return jnp.dot(vt_ref[lo:lo + HEAD_DIM, pl.ds(start, MOBA_BLOCK)], p.astype(BF16),
                       preferred_element_type=F32)

    parts = tq // MOBA_BLOCK
    first_own = i * parts
    own_starts = [pl.multiple_of((first_own + r) * MOBA_BLOCK, MOBA_BLOCK) for r in range(parts)]
    n_selectable = first_own + lax.broadcasted_iota(jnp.int32, (1, tq), 1) // MOBA_BLOCK
    key = lax.broadcasted_iota(jnp.int32, (MOBA_BLOCK, MOBA_BLOCK), 0)
    qry = lax.broadcasted_iota(jnp.int32, (MOBA_BLOCK, MOBA_BLOCK), 1)
    gates = [
        lax.dot_general(km_ref[0, :, h * HEAD_DIM:(h + 1) * HEAD_DIM],
                        q_ref[:, h * HEAD_DIM:(h + 1) * HEAD_DIM].astype(F32), nt,
                        precision=HIGHEST, preferred_element_type=F32)
        for h in range(HEADS)
    ]

    def own_scores(h, r):
        lo = h * HEAD_DIM
        kj = k_ref[0, pl.ds(own_starts[r], MOBA_BLOCK), lo:lo + HEAD_DIM]
        qr = q_ref[r * MOBA_BLOCK:(r + 1) * MOBA_BLOCK, lo:lo + HEAD_DIM]
        return lax.dot_general(kj, qr, nt, preferred_element_type=F32)

    own_ss = [[own_scores(h, r) for r in range(parts)] for h in range(HEADS)]
    ms, ls, ps = [], [], []
    for h in range(HEADS):
        bias_ref[h] = _top3_sublanes(gates[h], n_selectable)
        m_parts, l_parts, p_parts = [], [], []
        for r in range(parts):
            s = jnp.where(key <= qry, own_ss[h][r], NEG_INF)
            m = jnp.max(s, axis=0, keepdims=True)
            p = jnp.exp2((s - m) * c_exp)
            m_parts.append(m)
            l_parts.append(jnp.sum(p, axis=0, keepdims=True))
            p_parts.append(p)
        ms.append(jnp.concatenate(m_parts, axis=1))
        ls.append(jnp.concatenate(l_parts, axis=1))
        ps.append(p_parts)
    for h in range(HEADS):
        acc_ref[h] = jnp.concatenate(
            [weighted_values(h, own_starts[r], ps[h][r]) for r in range(parts)], axis=1)

    def body(j, carry):
        ms, ls = carry
        start = pl.multiple_of(j * MOBA_BLOCK, MOBA_BLOCK)
        ss = [scores(h, start) for h in range(HEADS)]
        new_ms, new_ls, alphas, ps = [], [], [], []
        for h in range(HEADS):
            s = ss[h] + bias_ref[h, pl.ds(j, 1), :]
            m_new = jnp.maximum(ms[h], jnp.max(s, axis=0, keepdims=True))
            alpha = jnp.exp2((ms[h] - m_new) * c_exp)
            p = jnp.exp2((s - m_new) * c_exp)
            new_ms.append(m_new)
            new_ls.append(alpha * ls[h] + jnp.sum(p, axis=0, keepdims=True))
            alphas.append(alpha)
            ps.append(p.astype(BF16))
        for h in range(HEADS):
            acc_ref[h] = alphas[h] * acc_ref[h] + weighted_values(h, start, ps[h])
        return tuple(new_ms), tuple(new_ls)

    _, ls = lax.fori_loop(0, first_own + parts - 1, body, (tuple(ms), tuple(ls)))
    for h in range(HEADS):
        o_ref[h * HEAD_DIM:(h + 1) * HEAD_DIM, :] = (acc_ref[h] / ls[h]).astype(BF16)


def _attn_prompt(q, k, vt, kmean):
    bn, t, _ = k.shape
    nb = t // MOBA_BLOCK
    tq = MOBA_QUERY_TILE
    nq = t // tq
    return pl.pallas_call(
        _attn_prompt_kernel,
        grid=(bn, nq),
        in_specs=[
            pl.BlockSpec((tq, D_MODEL), lambda b, i: (b * nq + i, 0)),
            pl.BlockSpec((1, t, D_MODEL), lambda b, i: (b, 0, 0), pipeline_mode=pl.Buffered(1)),
            pl.BlockSpec((D_MODEL, t), lambda b, i: (0, b), pipeline_mode=pl.Buffered(1)),
            pl.BlockSpec((1, nb, D_MODEL), lambda b, i: (b, 0, 0)),
        ],
        out_specs=pl.BlockSpec((D_MODEL, tq), lambda b, i: (0, b * nq + i)),
        out_shape=jax.ShapeDtypeStruct((D_MODEL, bn * t), BF16),
        scratch_shapes=[
            pltpu.VMEM((HEADS, nb, tq), F32),
            pltpu.VMEM((HEADS, HEAD_DIM, tq), F32),
        ],
        compiler_params=_params(("parallel", "arbitrary")),
        name="moba_prompt",
    )(q, k, vt, kmean)


def _attn_sample_kernel(pt_ref, q_ref, kn_ref, vn_ref, *rest, n_pages):
    kp_refs = rest[:n_pages]
    vp_refs = rest[n_pages:2 * n_pages]
    o_ref, kb_ref, vb_ref, s_ref = rest[2 * n_pages:]
    del pt_ref
    tq = q_ref.shape[1]
    past = n_pages * PAGE_SIZE
    n_past_blocks = past // MOBA_BLOCK
    n_new = 2 * SUBLANES
    c_exp = HEAD_DIM ** -0.5 * math.log2(math.e)

    q = q_ref[0].astype(F32)
    nr = HEADS * tq
    qt = jnp.concatenate([q] * HEADS, axis=0)
    rid = lax.broadcasted_iota(jnp.int32, (nr, D_MODEL), 0)
    lid = lax.broadcasted_iota(jnp.int32, (nr, D_MODEL), 1)
    qr = jnp.where(lid // HEAD_DIM == rid // tq, qt, 0.0).astype(BF16)

    nt = (((1,), (1,)), ((), ()))
    for p in range(n_pages):
        rows = slice(p * PAGE_SIZE, (p + 1) * PAGE_SIZE)
        for hd in range(HEADS):
            lo = hd * HEAD_DIM
            head_rows = pl.ds(hd, PAGE_SIZE, stride=HEADS)
            kb_ref[rows, lo:lo + HEAD_DIM] = kp_refs[p][0, head_rows, :].astype(BF16)
            vb_ref[rows, lo:lo + HEAD_DIM] = vp_refs[p][0, head_rows, :].astype(BF16)
        s_ref[:, rows] = lax.dot_general(qr, kb_ref[rows, :], nt, preferred_element_type=F32)
    zpad = jnp.zeros((n_new - tq, D_MODEL), F32)
    kb_ref[past:past + n_new, :] = jnp.concatenate([kn_ref[0], zpad], axis=0).astype(BF16)
    vb_ref[past:past + n_new, :] = jnp.concatenate([vn_ref[0], zpad], axis=0).astype(BF16)

    lane = lax.broadcasted_iota(jnp.int32, (nr, LANES), 1)
    gate = jnp.full((nr, LANES), NEG_INF, F32)
    for n in range(n_past_blocks):
        lo = n * MOBA_BLOCK
        block_mean = jnp.sum(s_ref[:, lo:lo + MOBA_BLOCK], axis=1, keepdims=True) * (1.0 / MOBA_BLOCK)
        gate = jnp.where(lane == n, block_mean, gate)
    bias = _top3_lanes(gate, n_past_blocks)

    s_new = lax.dot_general(qr, kb_ref[past:past + n_new, :], nt, preferred_element_type=F32)
    qi = lax.broadcasted_iota(jnp.int32, (nr, n_new), 0) % tq
    kr = lax.broadcasted_iota(jnp.int32, (nr, n_new), 1)
    s_new = jnp.where(kr <= qi, s_new, NEG_INF)
    m = jnp.max(s_new, axis=1, keepdims=True)
    for n in range(n_past_blocks):
        lo = n * MOBA_BLOCK
        sb = s_ref[:, lo:lo + MOBA_BLOCK] + bias[:, n:n + 1]
        s_ref[:, lo:lo + MOBA_BLOCK] = sb
        m = jnp.maximum(m, jnp.max(sb, axis=1, keepdims=True))
    p_new = jnp.exp2((s_new - m) * c_exp)
    l = jnp.sum(p_new, axis=1, keepdims=True)
    for n in range(n_past_blocks):
        lo = n * MOBA_BLOCK
        pb = jnp.exp2((s_ref[:, lo:lo + MOBA_BLOCK] - m) * c_exp)
        s_ref[:, lo:lo + MOBA_BLOCK] = pb
        l = l + jnp.sum(pb, axis=1, keepdims=True)
    inv_l = 1.0 / l
    out = jnp.dot((s_ref[:, :past] * inv_l).astype(BF16), vb_ref[:past, :], preferred_element_type=F32)
    out = out + jnp.dot((p_new * inv_l).astype(BF16), vb_ref[past:past + n_new, :], preferred_element_type=F32)
    for hd in range(HEADS):
        lo = hd * HEAD_DIM
        o_ref[0, :, lo:lo + HEAD_DIM] = out[hd * tq:(hd + 1) * tq, lo:lo + HEAD_DIM].astype(BF16)


def _attn_sample(q, k_new, v_new, cache_k, cache_v, page_table):
    ns, tq, _ = q.shape
    n_pages = page_table.shape[1]
    past = n_pages * PAGE_SIZE
    n_new = 2 * SUBLANES
    assert tq == SUBLANES and HEADS * tq <= LANES and past % MOBA_BLOCK == 0

    def page_spec(p):
        return pl.BlockSpec((1, PAGE_SIZE * HEADS, HEAD_DIM), lambda s, pt: (pt[s * n_pages + p], 0, 0))

    seq_spec = pl.BlockSpec((1, tq, D_MODEL), lambda s, pt: (s, 0, 0))
    grid_spec = pltpu.PrefetchScalarGridSpec(
        num_scalar_prefetch=1,
        grid=(ns,),
        in_specs=[seq_spec, seq_spec, seq_spec]
        + [page_spec(p) for p in range(n_pages)]
        + [page_spec(p) for p in range(n_pages)],
        out_specs=seq_spec,
        scratch_shapes=[
            pltpu.VMEM((past + n_new, D_MODEL), BF16),
            pltpu.VMEM((past + n_new, D_MODEL), BF16),
            pltpu.VMEM((HEADS * tq, past), F32),
        ],
    )
    return pl.pallas_call(
        functools.partial(_attn_sample_kernel, n_pages=n_pages),
        grid_spec=grid_spec,
        out_shape=jax.ShapeDtypeStruct((ns, tq, D_MODEL), BF16),
        compiler_params=_params(("arbitrary",)),
        name="moba_sample",
    )(page_table.reshape(-1), q, k_new, v_new, *([cache_k] * n_pages), *([cache_v] * n_pages))


def _ssd_chunk(raw, z, dtr, first_chunk, side_jobs, conv0_ref, ssm0_ref, wconv_ref, bconv_ref, dtb_ref, alog_ref,
               dskip_ref, ng_ref, tri_ref, y_ref, convo_ref, state_ref, ext_ref, yacc_ref):
    t = raw.shape[0]
    L = tri_ref.shape[0]
    halo = SUBLANES

    @pl.when(first_chunk)
    def _():
        ext_ref[0:halo, :] = conv0_ref[...]
        state_ref[...] = ssm0_ref[...]

    ext_ref[halo:halo + t, :] = raw
    if t < L:
        ext_ref[halo + t:halo + L, :] = jnp.zeros((L - t, SSM_CONV_DIM), F32)
    ext = ext_ref[...]
    conv = bconv_ref[...] + wconv_ref[SSM_CONV - 1:SSM_CONV, :] * ext[halo:, :]
    for back in range(1, SSM_CONV):
        w = SSM_CONV - 1 - back
        conv = conv + wconv_ref[w:w + 1, :] * pltpu.roll(ext, back, axis=0)[halo:, :]
    tail = ext_ref[t:t + halo, :]
    ext_ref[0:halo, :] = tail
    convo_ref[...] = tail
    xbc = _silu(conv)
    x = xbc[:, :SSM_D_INNER]
    bm = xbc[:, SSM_D_INNER:SSM_D_INNER + SSM_BC_DIM].astype(BF16)
    cm = xbc[:, SSM_D_INNER + SSM_BC_DIM:]

    dtx = dtr + dtb_ref[...]
    dt = jnp.maximum(dtx, 0.0) + jnp.log(1.0 + jnp.exp(-jnp.abs(dtx)))
    if t < L:
        dt = jnp.concatenate([dt, jnp.zeros((L - t, LANES), F32)], axis=0)
    a = -jnp.exp(alog_ref[...])
    adt = dt * a
    acs = jnp.dot(tri_ref[...], adt, precision=HIGHEST, preferred_element_type=F32)
    acs_t = acs.T
    dt_t = dt.T
    w_t = jnp.exp(acs_t[:, L - 1:L] - acs_t) * dt_t
    x_t = x.T
    xb = x.astype(BF16)

    row = lax.broadcasted_iota(jnp.int32, (L, L), 0)
    col = lax.broadcasted_iota(jnp.int32, (L, L), 1)
    causal = col <= row
    nt = (((1,), (1,)), ((), ()))
    for g in range(SSM_GROUPS):
        bg = bm[:, g * SSM_D_STATE:(g + 1) * SSM_D_STATE]
        cg = cm[:, g * SSM_D_STATE:(g + 1) * SSM_D_STATE]
        cb = lax.dot_general(cg.astype(BF16), bg, nt, preferred_element_type=F32)
        for e in range(SSM_HPG):
            hd = g * SSM_HPG + e
            lo = hd * SSM_HEAD_DIM
            acs_col = jnp.broadcast_to(acs[:, hd:hd + 1], (L, SSM_D_STATE))
            lmat = jnp.where(causal, jnp.exp(acs_col[:, :L] - acs_t[hd:hd + 1, :]), 0.0)
            mh = (cb * lmat * dt_t[hd:hd + 1, :]).astype(BF16)
            ch = (cg * jnp.exp(acs_col)).astype(BF16)
            sh = state_ref[lo:lo + SSM_HEAD_DIM, :]
            yh = jnp.dot(mh, xb[:, lo:lo + SSM_HEAD_DIM], preferred_element_type=F32)
            yh = yh + lax.dot_general(ch, sh.astype(BF16), nt, preferred_element_type=F32)
            yacc_ref[:, lo:lo + SSM_HEAD_DIM] = yh
            xw = (x_t[lo:lo + SSM_HEAD_DIM, :] * w_t[hd:hd + 1, :]).astype(BF16)
            upd = jnp.dot(xw, bg, preferred_element_type=F32)
            state_ref[lo:lo + SSM_HEAD_DIM, :] = jnp.exp(acs_t[hd:hd + 1, L - 1:L]) * sh + upd
            next(side_jobs, lambda: None)()
            yield

    y = yacc_ref[0:t, :] + x[:t] * dskip_ref[...]
    y = y * _silu(z)
    gw = SSM_D_INNER // SSM_GROUPS
    for g in range(SSM_GROUPS):
        yg = y[:, g * gw:(g + 1) * gw]
        yg = yg * lax.rsqrt(jnp.mean(yg * yg, axis=-1, keepdims=True) + RMS_EPS)
        y_ref[:, g * gw:(g + 1) * gw] = (yg * ng_ref[:, g * gw:(g + 1) * gw]).astype(y_ref.dtype)


SSM_PROJ_DIM = SSM_D_INNER + SSM_CONV_DIM + LANES
SSM_PROJ_COLS = 256


def _ssd_kernel(h_ref, hn_ref, conv0_ref, ssm0_ref, win_ref, *rest):
    *params, y_ref, convo_ref, ssmo_ref, ext_ref, yacc_ref, proj_ref, xn_ref = rest
    seqs, t, _ = h_ref.shape
    step = pl.program_id(0) * pl.num_programs(1) + pl.program_id(1)
    first_chunk = pl.program_id(1) == 0

    def project(x_ref, dst_ref):
        def piece(lo):
            hi = min(lo + SSM_PROJ_COLS, SSM_PROJ_DIM)

            def run():
                dst_ref[:, lo:hi] = jnp.dot(x_ref[...], win_ref[:, lo:hi], preferred_element_type=F32)
            return run
        return [piece(lo) for lo in range(0, SSM_PROJ_DIM, SSM_PROJ_COLS)]

    @pl.when(step == 0)
    def _():
        xn_ref[...] = h_ref[...].reshape(seqs * t, D_MODEL).astype(BF16)
        for job in project(xn_ref, proj_ref.at[0]):
            job()

    def run_step(cur_ref, nxt_ref):
        xn_ref[...] = hn_ref[...].reshape(seqs * t, D_MODEL).astype(BF16)
        side_jobs = iter(project(xn_ref, nxt_ref))
        chunks = []
        for s in range(seqs):
            rows = slice(s * t, (s + 1) * t)
            z = cur_ref[rows, :SSM_D_INNER]
            raw = cur_ref[rows, SSM_D_INNER:SSM_D_INNER + SSM_CONV_DIM]
            dtr = cur_ref[rows, SSM_D_INNER + SSM_CONV_DIM:]
            chunks.append(_ssd_chunk(
                raw, z, dtr, first_chunk, side_jobs, conv0_ref.at[s], ssm0_ref.at[s], *params,
                y_ref.at[s], convo_ref.at[s], ssmo_ref.at[s], ext_ref.at[s], yacc_ref.at[s]))
        while chunks:
            chunks = [c for c in chunks if next(c, "done") != "done"]
        for job in side_jobs:
            job()

    for parity in range(2):
        @pl.when(step % 2 == parity)
        def _():
            run_step(proj_ref.at[parity], proj_ref.at[1 - parity])


def _ssd(h, conv0, ssm0, w_in, w_conv, b_conv, dt_bias, a_log, d_skip, norm_g, *, seqs_per_step):
    bn, t_total, _ = h.shape
    t = min(t_total, SSM_CHUNK)
    nc = t_total // t
    chunk = max(t, 2 * SUBLANES)
    tri = jnp.tril(jnp.ones((chunk, chunk), F32))
    sq = seqs_per_step
    n_steps = (bn // sq) * nc
    blk = lambda width: pl.BlockSpec((sq, t, width), lambda b, c: (b, c, 0))
    per_seq = lambda rows, width: pl.BlockSpec((sq, rows, width), lambda b, c: (b, 0, 0))

    def next_step_rows(b, c):
        nxt = jnp.minimum(b * nc + c + 1, n_steps - 1)
        return nxt // nc, nxt % nc, 0

    return pl.pallas_call(
        _ssd_kernel,
        grid=(bn // sq, nc),
        in_specs=[
            blk(D_MODEL),
            pl.BlockSpec((sq, t, D_MODEL), next_step_rows),
            per_seq(SUBLANES, SSM_CONV_DIM),
            per_seq(SSM_D_INNER, SSM_D_STATE),
            _resident((D_MODEL, SSM_PROJ_DIM)),
            _resident((SUBLANES, SSM_CONV_DIM)),
            _resident((1, SSM_CONV_DIM)),
            _resident((1, LANES)),
            _resident((1, LANES)),
            _resident((1, SSM_D_INNER)),
            _resident((1, SSM_D_INNER)),
            _resident((chunk, chunk)),
        ],
        out_specs=[
            blk(SSM_D_INNER),
            per_seq(SUBLANES, SSM_CONV_DIM),
            per_seq(SSM_D_INNER, SSM_D_STATE),
        ],
        out_shape=[
            jax.ShapeDtypeStruct((bn, t_total, SSM_D_INNER), BF16),
            jax.ShapeDtypeStruct((bn, SUBLANES, SSM_CONV_DIM), F32),
            jax.ShapeDtypeStruct((bn, SSM_D_INNER, SSM_D_STATE), F32),
        ],
        scratch_shapes=[
            pltpu.VMEM((sq, SUBLANES + chunk, SSM_CONV_DIM), F32),
            pltpu.VMEM((sq, chunk, SSM_D_INNER), F32),
            pltpu.VMEM((2, sq * t, SSM_PROJ_DIM), F32),
            pltpu.VMEM((sq * t, D_MODEL), BF16),
        ],
        compiler_params=_params(("arbitrary", "arbitrary")),
        name="ssd",
    )(h, h, conv0, ssm0, w_in, w_conv, b_conv, dt_bias, a_log, d_skip, norm_g, tri)


def _rope_tables(pos):
    half = HEAD_DIM // 2
    inv = ROPE_THETA ** (-jnp.arange(half, dtype=F32) / half)
    ang = pos.astype(F32)[:, None] * inv[None, :]
    cos, sin = jnp.cos(ang), jnp.sin(ang)
    return jnp.concatenate([cos, cos], axis=-1), jnp.concatenate([-sin, sin], axis=-1)


def _row(v):
    return v.reshape(1, -1)


def kernel(x_prompt, x_sample, cache_k, cache_v, state_conv, state_ssm, page_table, ln_g, ln_b, ffn_w_in, ffn_w_out, cmlp_w_in, cmlp_ln_g, cmlp_ln_b, cmlp_w_s, cmlp_b_s, cmlp_w_out, moba_w_qkv, moba_w_out, ssm_w_in, ssm_w_conv, ssm_b_conv, ssm_dt_bias, ssm_a_log, ssm_d, ssm_norm_g, ssm_w_out):
    n_prompt, t_prompt, _ = x_prompt.shape
    n_sample, t_sample, _ = x_sample.shape
    n_pages = page_table.shape[1]
    past_len = n_pages * PAGE_SIZE
    hp = x_prompt.reshape(n_prompt * t_prompt, D_MODEL)
    hs = x_sample.reshape(n_sample * t_sample, D_MODEL)

    cos_p, sin_p = _rope_tables(jnp.arange(t_prompt))
    cos_p, sin_p = jnp.tile(cos_p, (n_prompt, 1)), jnp.tile(sin_p, (n_prompt, 1))
    cos_s, sin_s = _rope_tables(past_len + jnp.arange(t_sample))
    cos_s, sin_s = jnp.tile(cos_s, (n_sample, 1)), jnp.tile(sin_s, (n_sample, 1))

    outs = {k: [] for k in ("cmlp_v", "k_p", "v_p", "k_s", "v_s", "conv_p", "ssm_p", "conv_s", "ssm_s")}
    for i in range(DEPTH):
        w_in, w_out = ffn_w_in[i, 0].astype(BF16), ffn_w_out[i, 0].astype(BF16)
        hp, hs = _ffn(hp, hs, w_in, w_out, _row(ln_g[i, 0]), _row(ln_b[i, 0]))
        kind, j = i % 3, i // 3
        g1, b1 = _row(ln_g[i, 1]), _row(ln_b[i, 1])
        if kind == 0:
            samples_per_chunk = CMLP_CHUNK // t_sample
            w_sp_s = jnp.einsum("ab,gts->gatbs", jnp.eye(samples_per_chunk, dtype=F32),
                                cmlp_w_s[j][:, :t_sample, :t_sample]).reshape(CMLP_GROUPS, CMLP_CHUNK, CMLP_CHUNK)
            bias_p = jnp.repeat(cmlp_b_s[j].T, CMLP_DV // CMLP_GROUPS, axis=1)
            bias_s = jnp.tile(jnp.repeat(cmlp_b_s[j][:, :t_sample].T, CMLP_DV // CMLP_GROUPS, axis=1),
                              (samples_per_chunk, 1))
            a_par = (cmlp_w_in[j].astype(BF16), _row(cmlp_ln_g[j]), _row(cmlp_ln_b[j]))
            w_o = cmlp_w_out[j].astype(BF16)
            (hp,) = _cmlp(hp, *a_par, cmlp_w_s[j], bias_p, w_o, g1, b1, emit_v=False)
            hs, v_rows = _cmlp(hs, *a_par, w_sp_s, bias_s, w_o, g1, b1, emit_v=True)
            outs["cmlp_v"].append(v_rows.reshape(n_sample, t_sample, CMLP_DV))
        elif kind == 1:
            w_qkv = moba_w_qkv[j].astype(BF16)
            w_o = moba_w_out[j].astype(BF16)
            q, k, v, kb, vt, kmean = _qkv(hp, w_qkv, cos_p, sin_p, for_prompt=True)
            o_t = _attn_prompt(q, kb.reshape(n_prompt, t_prompt, D_MODEL), vt,
                               kmean.reshape(n_prompt, t_prompt // MOBA_BLOCK, D_MODEL))
            hp = _proj_ln(hp, o_t, w_o, g1, b1, pre_transposed=True)
            outs["k_p"].append(k.reshape(n_prompt, t_prompt, HEADS, HEAD_DIM))
            outs["v_p"].append(v.reshape(n_prompt, t_prompt, HEADS, HEAD_DIM))
            q, k, v = _qkv(hs, w_qkv, cos_s, sin_s, for_prompt=False)
            shs = (n_sample, t_sample, D_MODEL)
            pool = cache_k.shape[1]
            o = _attn_sample(q.reshape(shs), k.reshape(shs), v.reshape(shs),
                             cache_k.reshape(-1, PAGE_SIZE * HEADS, HEAD_DIM),
                             cache_v.reshape(-1, PAGE_SIZE * HEADS, HEAD_DIM), page_table + j * pool)
            hs = _proj_ln(hs, o.reshape(-1, D_MODEL), w_o, g1, b1)
            outs["k_s"].append(k.reshape(n_sample, t_sample, HEADS, HEAD_DIM))
            outs["v_s"].append(v.reshape(n_sample, t_sample, HEADS, HEAD_DIM))
        else:
            c_par = (
                jnp.pad(ssm_w_in[j], ((0, 0), (0, LANES - SSM_HEADS))).astype(BF16),
                jnp.pad(ssm_w_conv[j], ((0, SUBLANES - SSM_CONV), (0, 0))),
                _row(ssm_b_conv[j]),
                _row(jnp.pad(ssm_dt_bias[j], (0, LANES - SSM_HEADS))),
                _row(jnp.pad(ssm_a_log[j], (0, LANES - SSM_HEADS))),
                _row(jnp.repeat(ssm_d[j], SSM_HEAD_DIM)),
                _row(ssm_norm_g[j]),
            )
            w_o = ssm_w_out[j].astype(BF16)
            halo_pad = ((0, 0), (SUBLANES - (SSM_CONV - 1), 0), (0, 0))

            y, conv_t, ssm_f = _ssd(
                hp.reshape(n_prompt, t_prompt, D_MODEL),
                jnp.zeros((n_prompt, SUBLANES, SSM_CONV_DIM), F32),
                jnp.zeros((n_prompt, SSM_D_INNER, SSM_D_STATE), F32), *c_par, seqs_per_step=1)
            hp = _proj_ln(hp, y.reshape(-1, SSM_D_INNER), w_o, g1, b1)
            outs["conv_p"].append(conv_t[:, SUBLANES - (SSM_CONV - 1):])
            outs["ssm_p"].append(ssm_f.reshape(n_prompt, SSM_HEADS, SSM_HEAD_DIM, SSM_D_STATE))

            y, conv_t, ssm_f = _ssd(
                hs.reshape(n_sample, t_sample, D_MODEL),
                jnp.pad(state_conv[j], halo_pad),
                state_ssm[j].reshape(n_sample, SSM_D_INNER, SSM_D_STATE), *c_par,
                seqs_per_step=SSD_SAMPLE_SEQS)
            hs = _proj_ln(hs, y.reshape(-1, SSM_D_INNER), w_o, g1, b1)
            outs["conv_s"].append(conv_t[:, SUBLANES - (SSM_CONV - 1):])
            outs["ssm_s"].append(ssm_f.reshape(n_sample, SSM_HEADS, SSM_HEAD_DIM, SSM_D_STATE))
        w_in, w_out = ffn_w_in[i, 1].astype(BF16), ffn_w_out[i, 1].astype(BF16)
        hp, hs = _ffn(hp, hs, w_in, w_out, _row(ln_g[i, 2]), _row(ln_b[i, 2]))
    st = lambda name: jnp.stack(outs[name])
    return (hp.reshape(n_prompt, t_prompt, D_MODEL), hs.reshape(n_sample, t_sample, D_MODEL),
            st("cmlp_v"), st("k_p"), st("v_p"), st("k_s"), st("v_s"),
            st("conv_p"), st("ssm_p"), st("conv_s"), st("ssm_s"))
```

```python
import functools
import math

import jax
import jax.numpy as jnp
from jax import lax
from jax.experimental import pallas as pl
from jax.experimental.pallas import tpu as pltpu

F32 = jnp.float32
BF16 = jnp.bfloat16
HIGHEST = lax.Precision.HIGHEST

LANES = 128
SUBLANES = 8
VMEM_LIMIT_BYTES = 56 * 1024 * 1024

DEPTH = 4
D_MODEL = 1024
D_FF = 2816
DEEPNORM_ALPHA = (2 * DEPTH) ** 0.25
LN_EPS = 1e-5
RMS_EPS = 1e-5
FFN_HALF = 0.5

CMLP_CHUNK = 128
CMLP_GROUPS = 8
CMLP_DV = D_MODEL

HEAD_DIM = 128
HEADS = D_MODEL // HEAD_DIM
MOBA_BLOCK = 256
MOBA_TOPK = 3
MOBA_QUERY_TILE = 2 * MOBA_BLOCK
PAGE_SIZE = 128
ROPE_THETA = 10000.0

SSM_D_INNER = 2 * D_MODEL
SSM_HEAD_DIM = 64
SSM_HEADS = SSM_D_INNER // SSM_HEAD_DIM
SSM_GROUPS = 8
SSM_HPG = SSM_HEADS // SSM_GROUPS
SSM_D_STATE = 128
SSM_CONV = 4
SSM_BC_DIM = SSM_GROUPS * SSM_D_STATE
SSM_CONV_DIM = SSM_D_INNER + 2 * SSM_BC_DIM
SSM_CHUNK = 128
SSD_SAMPLE_SEQS = 4
SSM_STEP_CHUNKS = 2

NEG_INF = float("-inf")


def _params(semantics):
    return pltpu.CompilerParams(dimension_semantics=semantics, vmem_limit_bytes=VMEM_LIMIT_BYTES)


def _resident(shape):
    zeros = (0,) * len(shape)
    return pl.BlockSpec(shape, lambda *_: zeros, pipeline_mode=pl.Buffered(1))


def _layer_norm(y, g, b):
    mu = jnp.mean(y, axis=-1, keepdims=True)
    d = y - mu
    var = jnp.mean(d * d, axis=-1, keepdims=True)
    return d * lax.rsqrt(var + LN_EPS) * g + b


def _silu(x):
    return x * jax.nn.sigmoid(x)


FFN_COL_CHUNK = 256


def _ffn_tile(x_ref, win_ref, wout_ref, g_ref, b_ref, o_ref, act_ref):
    x = x_ref[...]
    xb = x.astype(BF16)
    for c in range(D_FF // FFN_COL_CHUNK):
        lo = c * FFN_COL_CHUNK
        gate = jnp.dot(xb, win_ref[:, lo:lo + FFN_COL_CHUNK], preferred_element_type=F32)
        up = jnp.dot(xb, win_ref[:, D_FF + lo:D_FF + lo + FFN_COL_CHUNK], preferred_element_type=F32)
        act_ref[:, lo:lo + FFN_COL_CHUNK] = (_silu(gate) * up).astype(BF16)
    y = jnp.dot(act_ref[...], wout_ref[...], preferred_element_type=F32)
    o_ref[...] = _layer_norm(DEEPNORM_ALPHA * x + FFN_HALF * y, g_ref[...], b_ref[...])


def _ffn_kernel(xp_ref, xs_ref, win_ref, wout_ref, g_ref, b_ref, op_ref, os_ref, act_ref, *, n_prompt_tiles):
    i = pl.program_id(0)

    @pl.when(i < n_prompt_tiles)
    def _():
        _ffn_tile(xp_ref, win_ref, wout_ref, g_ref, b_ref, op_ref, act_ref)

    @pl.when(i >= n_prompt_tiles)
    def _():
        _ffn_tile(xs_ref, win_ref, wout_ref, g_ref, b_ref, os_ref, act_ref)


def _ffn(hp, hs, w_in_all, w_out_all, layer, slot, g, b, *, tm=512):
    npt, nst = hp.shape[0] // tm, hs.shape[0] // tm
    prompt_spec = pl.BlockSpec((tm, D_MODEL), lambda i: (jnp.minimum(i, npt - 1), 0))
    sample_spec = pl.BlockSpec((tm, D_MODEL), lambda i: (jnp.maximum(i - npt, 0), 0))

    def weight_spec(rows, cols):
        return pl.BlockSpec((None, None, rows, cols), lambda i: (layer, slot, 0, 0), pipeline_mode=pl.Buffered(1))

    return pl.pallas_call(
        functools.partial(_ffn_kernel, n_prompt_tiles=npt),
        grid=(npt + nst,),
        in_specs=[
            prompt_spec,
            sample_spec,
            weight_spec(D_MODEL, 2 * D_FF),
            weight_spec(D_FF, D_MODEL),
            _resident((1, D_MODEL)),
            _resident((1, D_MODEL)),
        ],
        out_specs=[prompt_spec, sample_spec],
        out_shape=[jax.ShapeDtypeStruct(hp.shape, F32), jax.ShapeDtypeStruct(hs.shape, F32)],
        scratch_shapes=[pltpu.VMEM((tm, D_FF), BF16)],
        compiler_params=_params(("arbitrary",)),
        name="ffn",
    )(hp, hs, w_in_all, w_out_all, g, b)


def _proj_ln_kernel(h_ref, pre_ref, w_ref, g_ref, b_ref, o_ref, *, pre_transposed):
    contract_pre = 0 if pre_transposed else 1
    y = lax.dot_general(pre_ref[...], w_ref[...], (((contract_pre,), (0,)), ((), ())),
                        preferred_element_type=F32)
    o_ref[...] = _layer_norm(DEEPNORM_ALPHA * h_ref[...] + y, g_ref[...], b_ref[...])


def _proj_ln(h, pre, w_out, g, b, *, pre_transposed=False, tm=512):
    n = h.shape[0]
    k = w_out.shape[0]
    pre_spec = (pl.BlockSpec((k, tm), lambda i: (0, i)) if pre_transposed
                else pl.BlockSpec((tm, k), lambda i: (i, 0)))
    return pl.pallas_call(
        functools.partial(_proj_ln_kernel, pre_transposed=pre_transposed),
        grid=(n // tm,),
        in_specs=[
            pl.BlockSpec((tm, D_MODEL), lambda i: (i, 0)),
            pre_spec,
            _resident((k, D_MODEL)),
            _resident((1, D_MODEL)),
            _resident((1, D_MODEL)),
        ],
        out_specs=pl.BlockSpec((tm, D_MODEL), lambda i: (i, 0)),
        out_shape=jax.ShapeDtypeStruct((n, D_MODEL), F32),
        compiler_params=_params(("parallel",)),
        name="proj_ln",
    )(h, pre, w_out, g, b)


def _cmlp_kernel(h_ref, win_ref, lng_ref, lnb_ref, wsp_ref, bias_ref, wout_ref, g_ref, b_ref,
                 o_ref, *rest):
    *v_refs, pre_ref = rest
    tm = h_ref.shape[0]
    x = h_ref[...]
    uv = jnp.dot(x.astype(BF16), win_ref[...], preferred_element_type=F32)
    uv = 0.5 * uv * (1.0 + lax.erf(uv * math.sqrt(0.5)))
    u = uv[:, :CMLP_DV]
    v = _layer_norm(uv[:, CMLP_DV:], lng_ref[...], lnb_ref[...])
    for v_ref in v_refs:
        v_ref[...] = v
    vb = v.astype(BF16)
    row = lax.broadcasted_iota(jnp.int32, (CMLP_CHUNK, CMLP_CHUNK), 0)
    col = lax.broadcasted_iota(jnp.int32, (CMLP_CHUNK, CMLP_CHUNK), 1)
    causal = col <= row
    for g in range(CMLP_GROUPS):
        gl = g * LANES
        wg = jnp.where(causal, wsp_ref[g], 0.0).astype(BF16)
        n_chunks = tm // CMLP_CHUNK
        v_chunks = jnp.concatenate(
            [vb[c * CMLP_CHUNK:(c + 1) * CMLP_CHUNK, gl:gl + LANES] for c in range(n_chunks)], axis=1)
        s_chunks = jnp.dot(wg, v_chunks, preferred_element_type=F32)
        for c in range(n_chunks):
            cl = c * CMLP_CHUNK
            s = s_chunks[:, c * LANES:(c + 1) * LANES] + bias_ref[:, gl:gl + LANES]
            pre_ref[cl:cl + CMLP_CHUNK, gl:gl + LANES] = (u[cl:cl + CMLP_CHUNK, gl:gl + LANES] * s).astype(BF16)
    y = jnp.dot(pre_ref[...], wout_ref[...], preferred_element_type=F32)
    o_ref[...] = _layer_norm(DEEPNORM_ALPHA * x + y, g_ref[...], b_ref[...])


def _cmlp(h, w_in, ln_g, ln_b, w_sp, bias_full, w_out, g, b, *, emit_v, tm=512):
    n = h.shape[0]
    n_out = 2 if emit_v else 1
    return pl.pallas_call(
        _cmlp_kernel,
        grid=(n // tm,),
        in_specs=[
            pl.BlockSpec((tm, D_MODEL), lambda i: (i, 0)),
            _resident((D_MODEL, 2 * CMLP_DV)),
            _resident((1, CMLP_DV)),
            _resident((1, CMLP_DV)),
            _resident((CMLP_GROUPS, CMLP_CHUNK, CMLP_CHUNK)),
            _resident((CMLP_CHUNK, CMLP_DV)),
            _resident((CMLP_DV, D_MODEL)),
            _resident((1, D_MODEL)),
            _resident((1, D_MODEL)),
        ],
        out_specs=[pl.BlockSpec((tm, D_MODEL), lambda i: (i, 0))] * n_out,
        out_shape=[jax.ShapeDtypeStruct((n, D_MODEL), F32)] * n_out,
        scratch_shapes=[pltpu.VMEM((tm, CMLP_DV), BF16)],
        compiler_params=_params(("parallel",)),
        name="cmlp",
    )(h, w_in, ln_g, ln_b, w_sp, bias_full, w_out, g, b)


def _qkv_kernel(h_ref, w_ref, cos_ref, sin_ref, q_ref, k_ref, v_ref, *attn_refs):
    tm = h_ref.shape[0]
    qkv = jnp.dot(h_ref[...].astype(BF16), w_ref[...], preferred_element_type=F32)
    cos = cos_ref[...]
    sin = sin_ref[...]
    for hd in range(HEADS):
        lo = hd * HEAD_DIM
        qh = qkv[:, lo:lo + HEAD_DIM]
        q_ref[:, lo:lo + HEAD_DIM] = (qh * cos + pltpu.roll(qh, HEAD_DIM // 2, axis=1) * sin).astype(BF16)
        kh = qkv[:, D_MODEL + lo:D_MODEL + lo + HEAD_DIM]
        k_ref[:, lo:lo + HEAD_DIM] = kh * cos + pltpu.roll(kh, HEAD_DIM // 2, axis=1) * sin
    v = qkv[:, 2 * D_MODEL:]
    v_ref[...] = v
    if attn_refs:
        kb_ref, vt_ref, km_ref = attn_refs
        kb_ref[...] = k_ref[...].astype(BF16)
        vt_ref[...] = v.T.astype(BF16)
        for j in range(tm // MOBA_BLOCK):
            blk = k_ref[j * MOBA_BLOCK:(j + 1) * MOBA_BLOCK, :]
            km_ref[0, j:j + 1, :] = jnp.sum(blk, axis=0, keepdims=True) * (1.0 / MOBA_BLOCK)


def _qkv(h, w_qkv, cos, sin, *, for_prompt, tm=512):
    n = h.shape[0]
    nkb = tm // MOBA_BLOCK
    row_spec = lambda width: pl.BlockSpec((tm, width), lambda i: (i, 0))
    out_specs = [row_spec(D_MODEL)] * 3
    out_shape = [
        jax.ShapeDtypeStruct((n, D_MODEL), BF16),
        jax.ShapeDtypeStruct((n, D_MODEL), F32),
        jax.ShapeDtypeStruct((n, D_MODEL), F32),
    ]
    if for_prompt:
        out_specs += [
            row_spec(D_MODEL),
            pl.BlockSpec((D_MODEL, tm), lambda i: (0, i)),
            pl.BlockSpec((1, nkb, D_MODEL), lambda i: (i, 0, 0)),
        ]
        out_shape += [
            jax.ShapeDtypeStruct((n, D_MODEL), BF16),
            jax.ShapeDtypeStruct((D_MODEL, n), BF16),
            jax.ShapeDtypeStruct((n // tm, nkb, D_MODEL), F32),
        ]
    return pl.pallas_call(
        _qkv_kernel,
        grid=(n // tm,),
        in_specs=[
            row_spec(D_MODEL),
            _resident((D_MODEL, 3 * D_MODEL)),
            row_spec(HEAD_DIM),
            row_spec(HEAD_DIM),
        ],
        out_specs=out_specs,
        out_shape=out_shape,
        compiler_params=_params(("parallel",)),
        name="qkv_rope",
    )(h, w_qkv, cos, sin)


def _top3_sublanes(gate, n_valid):
    n_rows = gate.shape[0]
    blk = lax.broadcasted_iota(jnp.int32, gate.shape, 0)
    gate = jnp.where(blk < n_valid, gate, NEG_INF)
    blk = blk.astype(F32)
    bias = jnp.full(gate.shape, NEG_INF, F32)
    for _ in range(MOBA_TOPK):
        m = jnp.max(gate, axis=0, keepdims=True)
        first = jnp.min(jnp.where(gate == m, blk, float(n_rows)), axis=0, keepdims=True)
        pick = blk == first
        bias = jnp.where(pick & (m > NEG_INF), 0.0, bias)
        gate = jnp.where(pick, NEG_INF, gate)
    return bias


def _top3_lanes(gate, n_valid):
    lane = lax.broadcasted_iota(jnp.int32, gate.shape, 1)
    gate = jnp.where(lane < n_valid, gate, NEG_INF)
    bias = jnp.full(gate.shape, NEG_INF, F32)
    for _ in range(MOBA_TOPK):
        m = jnp.max(gate, axis=1, keepdims=True)
        first = jnp.min(jnp.where(gate == m, lane, LANES), axis=1, keepdims=True)
        pick = lane == first
        bias = jnp.where(pick & (m > NEG_INF), 0.0, bias)
        gate = jnp.where(pick, NEG_INF, gate)
    return bias


def _attn_prompt_kernel(q_ref, k_ref, vt_ref, km_ref, o_ref, bias_ref, acc_ref):
    i = pl.program_id(1)
    tq = q_ref.shape[0]
    c_exp = HEAD_DIM ** -0.5 * math.log2(math.e)
    nt = (((1,), (1,)), ((), ()))

    def scores(h, start):
        lo = h * HEAD_DIM
        kj = k_ref[0, pl.ds(start, MOBA_BLOCK), lo:lo + HEAD_DIM]
        return lax.dot_general(kj, q_ref[:, lo:lo + HEAD_DIM], nt, preferred_element_type=F32)

    def weighted_values(h, start, p):
        lo = h * HEAD_DIM
        return jnp.dot(vt_ref[lo:lo + HEAD_DIM, pl.ds(start, MOBA_BLOCK)], p.astype(BF16),
                       preferred_element_type=F32)

    parts = tq // MOBA_BLOCK
    first_own = i * parts
    own_starts = [pl.multiple_of((first_own + r) * MOBA_BLOCK, MOBA_BLOCK) for r in range(parts)]
    n_selectable = first_own + lax.broadcasted_iota(jnp.int32, (1, tq), 1) // MOBA_BLOCK
    key = lax.broadcasted_iota(jnp.int32, (MOBA_BLOCK, MOBA_BLOCK), 0)
    qry = lax.broadcasted_iota(jnp.int32, (MOBA_BLOCK, MOBA_BLOCK), 1)
    gates = [
        lax.dot_general(km_ref[0, :, h * HEAD_DIM:(h + 1) * HEAD_DIM],
                        q_ref[:, h * HEAD_DIM:(h + 1) * HEAD_DIM].astype(F32), nt,
                        precision=HIGHEST, preferred_element_type=F32)
        for h in range(HEADS)
    ]

    def own_scores(h, r):
        lo = h * HEAD_DIM
        kj = k_ref[0, pl.ds(own_starts[r], MOBA_BLOCK), lo:lo + HEAD_DIM]
        qr = q_ref[r * MOBA_BLOCK:(r + 1) * MOBA_BLOCK, lo:lo + HEAD_DIM]
        return lax.dot_general(kj, qr, nt, preferred_element_type=F32)

    own_ss = [[own_scores(h, r) for r in range(parts)] for h in range(HEADS)]
    ms, ls, ps = [], [], []
    for h in range(HEADS):
        bias_ref[h] = _top3_sublanes(gates[h], n_selectable)
        m_parts, l_parts, p_parts = [], [], []
        for r in range(parts):
            s = jnp.where(key <= qry, own_ss[h][r], NEG_INF)
            m = jnp.max(s, axis=0, keepdims=True)
            p = jnp.exp2((s - m) * c_exp)
            m_parts.append(m)
            l_parts.append(jnp.sum(p, axis=0, keepdims=True))
            p_parts.append(p)
        ms.append(jnp.concatenate(m_parts, axis=1))
        ls.append(jnp.concatenate(l_parts, axis=1))
        ps.append(p_parts)
    for h in range(HEADS):
        acc_ref[h] = jnp.concatenate(
            [weighted_values(h, own_starts[r], ps[h][r]) for r in range(parts)], axis=1)

    def body(j, carry):
        ms, ls = carry
        start = pl.multiple_of(j * MOBA_BLOCK, MOBA_BLOCK)
        ss = [scores(h, start) for h in range(HEADS)]
        new_ms, new_ls, alphas, ps = [], [], [], []
        for h in range(HEADS):
            s = ss[h] + bias_ref[h, pl.ds(j, 1), :]
            m_new = jnp.maximum(ms[h], jnp.max(s, axis=0, keepdims=True))
            alpha = jnp.exp2((ms[h] - m_new) * c_exp)
            p = jnp.exp2((s - m_new) * c_exp)
            new_ms.append(m_new)
            new_ls.append(alpha * ls[h] + jnp.sum(p, axis=0, keepdims=True))
            alphas.append(alpha)
            ps.append(p.astype(BF16))
        for h in range(HEADS):
            acc_ref[h] = alphas[h] * acc_ref[h] + weighted_values(h, start, ps[h])
        return tuple(new_ms), tuple(new_ls)

    _, ls = lax.fori_loop(0, first_own + parts - 1, body, (tuple(ms), tuple(ls)))
    for h in range(HEADS):
        o_ref[h * HEAD_DIM:(h + 1) * HEAD_DIM, :] = (acc_ref[h] / ls[h]).astype(BF16)


def _attn_prompt(q, k, vt, kmean):
    bn, t, _ = k.shape
    nb = t // MOBA_BLOCK
    tq = MOBA_QUERY_TILE
    nq = t // tq
    return pl.pallas_call(
        _attn_prompt_kernel,
        grid=(bn, nq),
        in_specs=[
            pl.BlockSpec((tq, D_MODEL), lambda b, i: (b * nq + i, 0)),
            pl.BlockSpec((1, t, D_MODEL), lambda b, i: (b, 0, 0), pipeline_mode=pl.Buffered(1)),
            pl.BlockSpec((D_MODEL, t), lambda b, i: (0, b), pipeline_mode=pl.Buffered(1)),
            pl.BlockSpec((1, nb, D_MODEL), lambda b, i: (b, 0, 0)),
        ],
        out_specs=pl.BlockSpec((D_MODEL, tq), lambda b, i: (0, b * nq + i)),
        out_shape=jax.ShapeDtypeStruct((D_MODEL, bn * t), BF16),
        scratch_shapes=[
            pltpu.VMEM((HEADS, nb, tq), F32),
            pltpu.VMEM((HEADS, HEAD_DIM, tq), F32),
        ],
        compiler_params=_params(("parallel", "arbitrary")),
        name="moba_prompt",
    )(q, k, vt, kmean)


def _attn_sample_kernel(pt_ref, q_ref, kn_ref, vn_ref, *rest, n_pages):
    kp_refs = rest[:n_pages]
    vp_refs = rest[n_pages:2 * n_pages]
    o_ref, kb_ref, vb_ref, s_ref = rest[2 * n_pages:]
    del pt_ref
    tq = q_ref.shape[1]
    past = n_pages * PAGE_SIZE
    n_past_blocks = past // MOBA_BLOCK
    n_new = 2 * SUBLANES
    c_exp = HEAD_DIM ** -0.5 * math.log2(math.e)

    q = q_ref[0].astype(F32)
    nr = HEADS * tq
    qt = jnp.concatenate([q] * HEADS, axis=0)
    rid = lax.broadcasted_iota(jnp.int32, (nr, D_MODEL), 0)
    lid = lax.broadcasted_iota(jnp.int32, (nr, D_MODEL), 1)
    qr = jnp.where(lid // HEAD_DIM == rid // tq, qt, 0.0).astype(BF16)

    nt = (((1,), (1,)), ((), ()))
    for p in range(n_pages):
        rows = slice(p * PAGE_SIZE, (p + 1) * PAGE_SIZE)
        for hd in range(HEADS):
            lo = hd * HEAD_DIM
            head_rows = pl.ds(hd, PAGE_SIZE, stride=HEADS)
            kb_ref[rows, lo:lo + HEAD_DIM] = kp_refs[p][0, head_rows, :].astype(BF16)
            vb_ref[rows, lo:lo + HEAD_DIM] = vp_refs[p][0, head_rows, :].astype(BF16)
        s_ref[:, rows] = lax.dot_general(qr, kb_ref[rows, :], nt, preferred_element_type=F32)
    zpad = jnp.zeros((n_new - tq, D_MODEL), F32)
    kb_ref[past:past + n_new, :] = jnp.concatenate([kn_ref[0], zpad], axis=0).astype(BF16)
    vb_ref[past:past + n_new, :] = jnp.concatenate([vn_ref[0], zpad], axis=0).astype(BF16)

    lane = lax.broadcasted_iota(jnp.int32, (nr, LANES), 1)
    gate = jnp.full((nr, LANES), NEG_INF, F32)
    for n in range(n_past_blocks):
        lo = n * MOBA_BLOCK
        block_mean = jnp.sum(s_ref[:, lo:lo + MOBA_BLOCK], axis=1, keepdims=True) * (1.0 / MOBA_BLOCK)
        gate = jnp.where(lane == n, block_mean, gate)
    bias = _top3_lanes(gate, n_past_blocks)

    s_new = lax.dot_general(qr, kb_ref[past:past + n_new, :], nt, preferred_element_type=F32)
    qi = lax.broadcasted_iota(jnp.int32, (nr, n_new), 0) % tq
    kr = lax.broadcasted_iota(jnp.int32, (nr, n_new), 1)
    s_new = jnp.where(kr <= qi, s_new, NEG_INF)
    m = jnp.max(s_new, axis=1, keepdims=True)
    for n in range(n_past_blocks):
        lo = n * MOBA_BLOCK
        sb = s_ref[:, lo:lo + MOBA_BLOCK] + bias[:, n:n + 1]
        s_ref[:, lo:lo + MOBA_BLOCK] = sb
        m = jnp.maximum(m, jnp.max(sb, axis=1, keepdims=True))
    p_new = jnp.exp2((s_new - m) * c_exp)
    l = jnp.sum(p_new, axis=1, keepdims=True)
    for n in range(n_past_blocks):
        lo = n * MOBA_BLOCK
        pb = jnp.exp2((s_ref[:, lo:lo + MOBA_BLOCK] - m) * c_exp)
        s_ref[:, lo:lo + MOBA_BLOCK] = pb
        l = l + jnp.sum(pb, axis=1, keepdims=True)
    inv_l = 1.0 / l
    out = jnp.dot((s_ref[:, :past] * inv_l).astype(BF16), vb_ref[:past, :], preferred_element_type=F32)
    out = out + jnp.dot((p_new * inv_l).astype(BF16), vb_ref[past:past + n_new, :], preferred_element_type=F32)
    for hd in range(HEADS):
        lo = hd * HEAD_DIM
        o_ref[0, :, lo:lo + HEAD_DIM] = out[hd * tq:(hd + 1) * tq, lo:lo + HEAD_DIM].astype(BF16)


def _attn_sample(q, k_new, v_new, cache_k, cache_v, page_table):
    ns, tq, _ = q.shape
    n_pages = page_table.shape[1]
    past = n_pages * PAGE_SIZE
    n_new = 2 * SUBLANES
    assert tq == SUBLANES and HEADS * tq <= LANES and past % MOBA_BLOCK == 0

    def page_spec(p):
        return pl.BlockSpec((1, PAGE_SIZE * HEADS, HEAD_DIM), lambda s, pt: (pt[s * n_pages + p], 0, 0))

    seq_spec = pl.BlockSpec((1, tq, D_MODEL), lambda s, pt: (s, 0, 0))
    grid_spec = pltpu.PrefetchScalarGridSpec(
        num_scalar_prefetch=1,
        grid=(ns,),
        in_specs=[seq_spec, seq_spec, seq_spec]
        + [page_spec(p) for p in range(n_pages)]
        + [page_spec(p) for p in range(n_pages)],
        out_specs=seq_spec,
        scratch_shapes=[
            pltpu.VMEM((past + n_new, D_MODEL), BF16),
            pltpu.VMEM((past + n_new, D_MODEL), BF16),
            pltpu.VMEM((HEADS * tq, past), F32),
        ],
    )
    return pl.pallas_call(
        functools.partial(_attn_sample_kernel, n_pages=n_pages),
        grid_spec=grid_spec,
        out_shape=jax.ShapeDtypeStruct((ns, tq, D_MODEL), BF16),
        compiler_params=_params(("arbitrary",)),
        name="moba_sample",
    )(page_table.reshape(-1), q, k_new, v_new, *([cache_k] * n_pages), *([cache_v] * n_pages))


def _ssd_chunk(raw, z, dtr, first_chunk, side_jobs, conv0_ref, ssm0_ref, wconv_ref, bconv_ref, dtb_ref, alog_ref,
               dskip_ref, ng_ref, tri_ref, y_ref, convo_ref, state_ref, ext_ref, yacc_ref):
    t = raw.shape[0]
    L = tri_ref.shape[0]
    halo = SUBLANES

    if first_chunk is not None:
        @pl.when(first_chunk)
        def _():
            ext_ref[0:halo, :] = conv0_ref[...]
            state_ref[...] = ssm0_ref[...]

    ext_ref[halo:halo + t, :] = raw
    if t < L:
        ext_ref[halo + t:halo + L, :] = jnp.zeros((L - t, SSM_CONV_DIM), F32)
    ext = ext_ref[...]
    conv = bconv_ref[...] + wconv_ref[SSM_CONV - 1:SSM_CONV, :] * ext[halo:, :]
    for back in range(1, SSM_CONV):
        w = SSM_CONV - 1 - back
        conv = conv + wconv_ref[w:w + 1, :] * pltpu.roll(ext, back, axis=0)[halo:, :]
    tail = ext_ref[t:t + halo, :]
    ext_ref[0:halo, :] = tail
    convo_ref[...] = tail
    xbc = _silu(conv)
    x = xbc[:, :SSM_D_INNER]
    bm = xbc[:, SSM_D_INNER:SSM_D_INNER + SSM_BC_DIM].astype(BF16)
    cm = xbc[:, SSM_D_INNER + SSM_BC_DIM:]

    dtx = dtr + dtb_ref[...]
    dt = jnp.maximum(dtx, 0.0) + jnp.log(1.0 + jnp.exp(-jnp.abs(dtx)))
    if t < L:
        dt = jnp.concatenate([dt, jnp.zeros((L - t, LANES), F32)], axis=0)
    a = -jnp.exp(alog_ref[...])
    adt = dt * a
    acs = jnp.dot(tri_ref[...], adt, precision=HIGHEST, preferred_element_type=F32)
    acs_t = acs.T
    dt_t = dt.T
    w_t = jnp.exp(acs_t[:, L - 1:L] - acs_t) * dt_t
    x_t = x.T
    xb = x.astype(BF16)

    row = lax.broadcasted_iota(jnp.int32, (L, L), 0)
    col = lax.broadcasted_iota(jnp.int32, (L, L), 1)
    causal = col <= row
    nt = (((1,), (1,)), ((), ()))
    for g in range(SSM_GROUPS):
        bg = bm[:, g * SSM_D_STATE:(g + 1) * SSM_D_STATE]
        cg = cm[:, g * SSM_D_STATE:(g + 1) * SSM_D_STATE]
        cb = lax.dot_general(cg.astype(BF16), bg, nt, preferred_element_type=F32)
        for e in range(SSM_HPG):
            hd = g * SSM_HPG + e
            lo = hd * SSM_HEAD_DIM
            acs_col = jnp.broadcast_to(acs[:, hd:hd + 1], (L, SSM_D_STATE))
            lmat = jnp.where(causal, jnp.exp(acs_col[:, :L] - acs_t[hd:hd + 1, :]), 0.0)
            mh = (cb * lmat * dt_t[hd:hd + 1, :]).astype(BF16)
            ch = (cg * jnp.exp(acs_col)).astype(BF16)
            sh = state_ref[lo:lo + SSM_HEAD_DIM, :]
            yh = jnp.dot(mh, xb[:, lo:lo + SSM_HEAD_DIM], preferred_element_type=F32)
            yh = yh + lax.dot_general(ch, sh.astype(BF16), nt, preferred_element_type=F32)
            yacc_ref[:, lo:lo + SSM_HEAD_DIM] = yh
            xw = (x_t[lo:lo + SSM_HEAD_DIM, :] * w_t[hd:hd + 1, :]).astype(BF16)
            upd = jnp.dot(xw, bg, preferred_element_type=F32)
            state_ref[lo:lo + SSM_HEAD_DIM, :] = jnp.exp(acs_t[hd:hd + 1, L - 1:L]) * sh + upd
            next(side_jobs, lambda: None)()
            yield

    y = yacc_ref[0:t, :] + x[:t] * dskip_ref[...]
    y = y * _silu(z)
    gw = SSM_D_INNER // SSM_GROUPS
    for g in range(SSM_GROUPS):
        yg = y[:, g * gw:(g + 1) * gw]
        yg = yg * lax.rsqrt(jnp.mean(yg * yg, axis=-1, keepdims=True) + RMS_EPS)
        y_ref[:, g * gw:(g + 1) * gw] = (yg * ng_ref[:, g * gw:(g + 1) * gw]).astype(y_ref.dtype)


SSM_PROJ_DIM = SSM_D_INNER + SSM_CONV_DIM + LANES
SSM_PROJ_COLS = 256


def _ssd_kernel(h_ref, hn_ref, conv0_ref, ssm0_ref, win_ref, *rest):
    *params, y_ref, convo_ref, ssmo_ref, ext_ref, yacc_ref, proj_ref, xn_ref = rest
    seqs, t, _ = h_ref.shape
    step = pl.program_id(0) * pl.num_programs(1) + pl.program_id(1)
    first_chunk = pl.program_id(1) == 0

    def project(x_ref, dst_ref):
        def piece(lo):
            hi = min(lo + SSM_PROJ_COLS, SSM_PROJ_DIM)

            def run():
                dst_ref[:, lo:hi] = jnp.dot(x_ref[...], win_ref[:, lo:hi], preferred_element_type=F32)
            return run
        return [piece(lo) for lo in range(0, SSM_PROJ_DIM, SSM_PROJ_COLS)]

    @pl.when(step == 0)
    def _():
        xn_ref[...] = h_ref[...].reshape(seqs * t, D_MODEL).astype(BF16)
        for job in project(xn_ref, proj_ref.at[0]):
            job()

    def run_step(cur_ref, nxt_ref):
        xn_ref[...] = hn_ref[...].reshape(seqs * t, D_MODEL).astype(BF16)
        side_jobs = iter(project(xn_ref, nxt_ref))
        chunk_rows = min(t, yacc_ref.shape[1])

        def sequence(s):
            for k in range(t // chunk_rows):
                rows = slice(s * t + k * chunk_rows, s * t + (k + 1) * chunk_rows)
                z = cur_ref[rows, :SSM_D_INNER]
                raw = cur_ref[rows, SSM_D_INNER:SSM_D_INNER + SSM_CONV_DIM]
                dtr = cur_ref[rows, SSM_D_INNER + SSM_CONV_DIM:]
                yield from _ssd_chunk(
                    raw, z, dtr, first_chunk if k == 0 else None, side_jobs, conv0_ref.at[s], ssm0_ref.at[s],
                    *params, y_ref.at[s, k * chunk_rows:(k + 1) * chunk_rows], convo_ref.at[s], ssmo_ref.at[s],
                    ext_ref.at[s], yacc_ref.at[s])

        active = [sequence(s) for s in range(seqs)]
        while active:
            active = [c for c in active if next(c, "done") != "done"]
        for job in side_jobs:
            job()

    for parity in range(2):
        @pl.when(step % 2 == parity)
        def _():
            run_step(proj_ref.at[parity], proj_ref.at[1 - parity])


def _ssd(h, conv0, ssm0, w_in, w_conv, b_conv, dt_bias, a_log, d_skip, norm_g, *, seqs_per_step):
    bn, t_total, _ = h.shape
    t = min(t_total, SSM_STEP_CHUNKS * SSM_CHUNK)
    nc = t_total // t
    chunk = max(min(t, SSM_CHUNK), 2 * SUBLANES)
    tri = jnp.tril(jnp.ones((chunk, chunk), F32))
    sq = seqs_per_step
    n_steps = (bn // sq) * nc
    blk = lambda width: pl.BlockSpec((sq, t, width), lambda b, c: (b, c, 0))
    per_seq = lambda rows, width: pl.BlockSpec((sq, rows, width), lambda b, c: (b, 0, 0))

    def next_step_rows(b, c):
        nxt = jnp.minimum(b * nc + c + 1, n_steps - 1)
        return nxt // nc, nxt % nc, 0

    return pl.pallas_call(
        _ssd_kernel,
        grid=(bn // sq, nc),
        in_specs=[
            blk(D_MODEL),
            pl.BlockSpec((sq, t, D_MODEL), next_step_rows),
            per_seq(SUBLANES, SSM_CONV_DIM),
            per_seq(SSM_D_INNER, SSM_D_STATE),
            _resident((D_MODEL, SSM_PROJ_DIM)),
            _resident((SUBLANES, SSM_CONV_DIM)),
            _resident((1, SSM_CONV_DIM)),
            _resident((1, LANES)),
            _resident((1, LANES)),
            _resident((1, SSM_D_INNER)),
            _resident((1, SSM_D_INNER)),
            _resident((chunk, chunk)),
        ],
        out_specs=[
            blk(SSM_D_INNER),
            per_seq(SUBLANES, SSM_CONV_DIM),
            per_seq(SSM_D_INNER, SSM_D_STATE),
        ],
        out_shape=[
            jax.ShapeDtypeStruct((bn, t_total, SSM_D_INNER), BF16),
            jax.ShapeDtypeStruct((bn, SUBLANES, SSM_CONV_DIM), F32),
            jax.ShapeDtypeStruct((bn, SSM_D_INNER, SSM_D_STATE), F32),
        ],
        scratch_shapes=[
            pltpu.VMEM((sq, SUBLANES + chunk, SSM_CONV_DIM), F32),
            pltpu.VMEM((sq, chunk, SSM_D_INNER), F32),
            pltpu.VMEM((2, sq * t, SSM_PROJ_DIM), F32),
            pltpu.VMEM((sq * t, D_MODEL), BF16),
        ],
        compiler_params=_params(("arbitrary", "arbitrary")),
        name="ssd",
    )(h, h, conv0, ssm0, w_in, w_conv, b_conv, dt_bias, a_log, d_skip, norm_g, tri)


def _rope_tables(pos):
    half = HEAD_DIM // 2
    inv = ROPE_THETA ** (-jnp.arange(half, dtype=F32) / half)
    ang = pos.astype(F32)[:, None] * inv[None, :]
    cos, sin = jnp.cos(ang), jnp.sin(ang)
    return jnp.concatenate([cos, cos], axis=-1), jnp.concatenate([-sin, sin], axis=-1)


def _row(v):
    return v.reshape(1, -1)


def kernel(x_prompt, x_sample, cache_k, cache_v, state_conv, state_ssm, page_table, ln_g, ln_b, ffn_w_in, ffn_w_out, cmlp_w_in, cmlp_ln_g, cmlp_ln_b, cmlp_w_s, cmlp_b_s, cmlp_w_out, moba_w_qkv, moba_w_out, ssm_w_in, ssm_w_conv, ssm_b_conv, ssm_dt_bias, ssm_a_log, ssm_d, ssm_norm_g, ssm_w_out):
    n_prompt, t_prompt, _ = x_prompt.shape
    n_sample, t_sample, _ = x_sample.shape
    n_pages = page_table.shape[1]
    past_len = n_pages * PAGE_SIZE
    hp = x_prompt.reshape(n_prompt * t_prompt, D_MODEL)
    hs = x_sample.reshape(n_sample * t_sample, D_MODEL)

    cos_p, sin_p = _rope_tables(jnp.arange(t_prompt))
    cos_p, sin_p = jnp.tile(cos_p, (n_prompt, 1)), jnp.tile(sin_p, (n_prompt, 1))
    cos_s, sin_s = _rope_tables(past_len + jnp.arange(t_sample))
    cos_s, sin_s = jnp.tile(cos_s, (n_sample, 1)), jnp.tile(sin_s, (n_sample, 1))

    outs = {k: [] for k in ("cmlp_v", "k_p", "v_p", "k_s", "v_s", "conv_p", "ssm_p", "conv_s", "ssm_s")}
    ffn_w_in_bf, ffn_w_out_bf = ffn_w_in.astype(BF16), ffn_w_out.astype(BF16)
    for i in range(DEPTH):
        hp, hs = _ffn(hp, hs, ffn_w_in_bf, ffn_w_out_bf, i, 0, _row(ln_g[i, 0]), _row(ln_b[i, 0]))
        kind, j = i % 3, i // 3
        g1, b1 = _row(ln_g[i, 1]), _row(ln_b[i, 1])
        if kind == 0:
            samples_per_chunk = CMLP_CHUNK // t_sample
            w_sp_s = jnp.einsum("ab,gts->gatbs", jnp.eye(samples_per_chunk, dtype=F32),
                                cmlp_w_s[j][:, :t_sample, :t_sample]).reshape(CMLP_GROUPS, CMLP_CHUNK, CMLP_CHUNK)
            bias_p = jnp.repeat(cmlp_b_s[j].T, CMLP_DV // CMLP_GROUPS, axis=1)
            bias_s = jnp.tile(jnp.repeat(cmlp_b_s[j][:, :t_sample].T, CMLP_DV // CMLP_GROUPS, axis=1),
                              (samples_per_chunk, 1))
            a_par = (cmlp_w_in[j].astype(BF16), _row(cmlp_ln_g[j]), _row(cmlp_ln_b[j]))
            w_o = cmlp_w_out[j].astype(BF16)
            (hp,) = _cmlp(hp, *a_par, cmlp_w_s[j], bias_p, w_o, g1, b1, emit_v=False)
            hs, v_rows = _cmlp(hs, *a_par, w_sp_s, bias_s, w_o, g1, b1, emit_v=True)
            outs["cmlp_v"].append(v_rows.reshape(n_sample, t_sample, CMLP_DV))
        elif kind == 1:
            w_qkv = moba_w_qkv[j].astype(BF16)
            w_o = moba_w_out[j].astype(BF16)
            q, k, v, kb, vt, kmean = _qkv(hp, w_qkv, cos_p, sin_p, for_prompt=True)
            o_t = _attn_prompt(q, kb.reshape(n_prompt, t_prompt, D_MODEL), vt,
                               kmean.reshape(n_prompt, t_prompt // MOBA_BLOCK, D_MODEL))
            hp = _proj_ln(hp, o_t, w_o, g1, b1, pre_transposed=True)
            outs["k_p"].append(k.reshape(n_prompt, t_prompt, HEADS, HEAD_DIM))
            outs["v_p"].append(v.reshape(n_prompt, t_prompt, HEADS, HEAD_DIM))
            q, k, v = _qkv(hs, w_qkv, cos_s, sin_s, for_prompt=False)
            shs = (n_sample, t_sample, D_MODEL)
            pool = cache_k.shape[1]
            o = _attn_sample(q.reshape(shs), k.reshape(shs), v.reshape(shs),
                             cache_k.reshape(-1, PAGE_SIZE * HEADS, HEAD_DIM),
                             cache_v.reshape(-1, PAGE_SIZE * HEADS, HEAD_DIM), page_table + j * pool)
            hs = _proj_ln(hs, o.reshape(-1, D_MODEL), w_o, g1, b1)
            outs["k_s"].append(k.reshape(n_sample, t_sample, HEADS, HEAD_DIM))
            outs["v_s"].append(v.reshape(n_sample, t_sample, HEADS, HEAD_DIM))
        else:
            c_par = (
                jnp.pad(ssm_w_in[j].astype(BF16), ((0, 0), (0, LANES - SSM_HEADS))),
                jnp.pad(ssm_w_conv[j], ((0, SUBLANES - SSM_CONV), (0, 0))),
                _row(ssm_b_conv[j]),
                _row(jnp.pad(ssm_dt_bias[j], (0, LANES - SSM_HEADS))),
                _row(jnp.pad(ssm_a_log[j], (0, LANES - SSM_HEADS))),
                _row(jnp.repeat(ssm_d[j], SSM_HEAD_DIM)),
                _row(ssm_norm_g[j]),
            )
            w_o = ssm_w_out[j].astype(BF16)
            halo_pad = ((0, 0), (SUBLANES - (SSM_CONV - 1), 0), (0, 0))

            y, conv_t, ssm_f = _ssd(
                hp.reshape(n_prompt, t_prompt, D_MODEL),
                jnp.zeros((n_prompt, SUBLANES, SSM_CONV_DIM), F32),
                jnp.zeros((n_prompt, SSM_D_INNER, SSM_D_STATE), F32), *c_par, seqs_per_step=1)
            hp = _proj_ln(hp, y.reshape(-1, SSM_D_INNER), w_o, g1, b1)
            outs["conv_p"].append(conv_t[:, SUBLANES - (SSM_CONV - 1):])
            outs["ssm_p"].append(ssm_f.reshape(n_prompt, SSM_HEADS, SSM_HEAD_DIM, SSM_D_STATE))

            y, conv_t, ssm_f = _ssd(
                hs.reshape(n_sample, t_sample, D_MODEL),
                jnp.pad(state_conv[j], halo_pad),
                state_ssm[j].reshape(n_sample, SSM_D_INNER, SSM_D_STATE), *c_par,
                seqs_per_step=SSD_SAMPLE_SEQS)
            hs = _proj_ln(hs, y.reshape(-1, SSM_D_INNER), w_o, g1, b1)
            outs["conv_s"].append(conv_t[:, SUBLANES - (SSM_CONV - 1):])
            outs["ssm_s"].append(ssm_f.reshape(n_sample, SSM_HEADS, SSM_HEAD_DIM, SSM_D_STATE))
        hp, hs = _ffn(hp, hs, ffn_w_in_bf, ffn_w_out_bf, i, 1, _row(ln_g[i, 2]), _row(ln_b[i, 2]))
    st = lambda name: jnp.stack(outs[name])
    return (hp.reshape(n_prompt, t_prompt, D_MODEL), hs.reshape(n_sample, t_sample, D_MODEL),
            st("cmlp_v"), st("k_p"), st("v_p"), st("k_s"), st("v_s"),
            st("conv_p"), st("ssm_p"), st("conv_s"), st("ssm_s"))
```

```python
import functools
import math

import jax
import jax.numpy as jnp
from jax import lax
from jax.experimental import pallas as pl
from jax.experimental.pallas import tpu as pltpu

F32 = jnp.float32
BF16 = jnp.bfloat16
HIGHEST = lax.Precision.HIGHEST

LANES = 128
SUBLANES = 8
VMEM_LIMIT_BYTES = 56 * 1024 * 1024

DEPTH = 4
D_MODEL = 1024
D_FF = 2816
DEEPNORM_ALPHA = (2 * DEPTH) ** 0.25
LN_EPS = 1e-5
RMS_EPS = 1e-5
FFN_HALF = 0.5

CMLP_CHUNK = 128
CMLP_GROUPS = 8
CMLP_DV = D_MODEL

HEAD_DIM = 128
HEADS = D_MODEL // HEAD_DIM
MOBA_BLOCK = 256
MOBA_TOPK = 3
MOBA_QUERY_TILE = 2 * MOBA_BLOCK
PAGE_SIZE = 128
ROPE_THETA = 10000.0

SSM_D_INNER = 2 * D_MODEL
SSM_HEAD_DIM = 64
SSM_HEADS = SSM_D_INNER // SSM_HEAD_DIM
SSM_GROUPS = 8
SSM_HPG = SSM_HEADS // SSM_GROUPS
SSM_D_STATE = 128
SSM_CONV = 4
SSM_BC_DIM = SSM_GROUPS * SSM_D_STATE
SSM_CONV_DIM = SSM_D_INNER + 2 * SSM_BC_DIM
SSM_CHUNK = 128
SSD_SAMPLE_SEQS = 4
SSM_STEP_CHUNKS = 2

NEG_INF = float("-inf")


def _params(semantics):
    return pltpu.CompilerParams(dimension_semantics=semantics, vmem_limit_bytes=VMEM_LIMIT_BYTES)


def _resident(shape):
    zeros = (0,) * len(shape)
    return pl.BlockSpec(shape, lambda *_: zeros, pipeline_mode=pl.Buffered(1))


def _layer_norm(y, g, b):
    mu = jnp.mean(y, axis=-1, keepdims=True)
    d = y - mu
    var = jnp.mean(d * d, axis=-1, keepdims=True)
    return d * lax.rsqrt(var + LN_EPS) * g + b


def _silu(x):
    return x * jax.nn.sigmoid(x)


FFN_COL_CHUNK = 256


def _ffn_tile(x_ref, win_ref, wout_ref, g_ref, b_ref, o_ref, act_ref):
    x = x_ref[...]
    xb = x.astype(BF16)
    for c in range(D_FF // FFN_COL_CHUNK):
        lo = c * FFN_COL_CHUNK
        gate = jnp.dot(xb, win_ref[:, lo:lo + FFN_COL_CHUNK], preferred_element_type=F32)
        up = jnp.dot(xb, win_ref[:, D_FF + lo:D_FF + lo + FFN_COL_CHUNK], preferred_element_type=F32)
        act_ref[:, lo:lo + FFN_COL_CHUNK] = (_silu(gate) * up).astype(BF16)
    y = jnp.dot(act_ref[...], wout_ref[...], preferred_element_type=F32)
    o_ref[...] = _layer_norm(DEEPNORM_ALPHA * x + FFN_HALF * y, g_ref[...], b_ref[...])


def _ffn_kernel(xp_ref, xs_ref, win_ref, wout_ref, g_ref, b_ref, op_ref, os_ref, act_ref, *, n_prompt_tiles):
    i = pl.program_id(0)

    @pl.when(i < n_prompt_tiles)
    def _():
        _ffn_tile(xp_ref, win_ref, wout_ref, g_ref, b_ref, op_ref, act_ref)

    @pl.when(i >= n_prompt_tiles)
    def _():
        _ffn_tile(xs_ref, win_ref, wout_ref, g_ref, b_ref, os_ref, act_ref)


def _ffn(hp, hs, w_in_all, w_out_all, layer, slot, g, b, *, tm=512):
    npt, nst = hp.shape[0] // tm, hs.shape[0] // tm
    prompt_spec = pl.BlockSpec((tm, D_MODEL), lambda i: (jnp.minimum(i, npt - 1), 0))
    sample_spec = pl.BlockSpec((tm, D_MODEL), lambda i: (jnp.maximum(i - npt, 0), 0))

    def weight_spec(rows, cols):
        return pl.BlockSpec((None, None, rows, cols), lambda i: (layer, slot, 0, 0), pipeline_mode=pl.Buffered(1))

    return pl.pallas_call(
        functools.partial(_ffn_kernel, n_prompt_tiles=npt),
        grid=(npt + nst,),
        in_specs=[
            prompt_spec,
            sample_spec,
            weight_spec(D_MODEL, 2 * D_FF),
            weight_spec(D_FF, D_MODEL),
            _resident((1, D_MODEL)),
            _resident((1, D_MODEL)),
        ],
        out_specs=[prompt_spec, sample_spec],
        out_shape=[jax.ShapeDtypeStruct(hp.shape, F32), jax.ShapeDtypeStruct(hs.shape, F32)],
        scratch_shapes=[pltpu.VMEM((tm, D_FF), BF16)],
        compiler_params=_params(("arbitrary",)),
        name="ffn",
    )(hp, hs, w_in_all, w_out_all, g, b)


def _proj_ln_kernel(h_ref, pre_ref, w_ref, g_ref, b_ref, o_ref, *, pre_transposed):
    contract_pre = 0 if pre_transposed else 1
    y = lax.dot_general(pre_ref[...], w_ref[...], (((contract_pre,), (0,)), ((), ())),
                        preferred_element_type=F32)
    o_ref[...] = _layer_norm(DEEPNORM_ALPHA * h_ref[...] + y, g_ref[...], b_ref[...])


def _proj_ln(h, pre, w_out, g, b, *, pre_transposed=False, tm=512):
    n = h.shape[0]
    k = w_out.shape[0]
    pre_spec = (pl.BlockSpec((k, tm), lambda i: (0, i)) if pre_transposed
                else pl.BlockSpec((tm, k), lambda i: (i, 0)))
    return pl.pallas_call(
        functools.partial(_proj_ln_kernel, pre_transposed=pre_transposed),
        grid=(n // tm,),
        in_specs=[
            pl.BlockSpec((tm, D_MODEL), lambda i: (i, 0)),
            pre_spec,
            _resident((k, D_MODEL)),
            _resident((1, D_MODEL)),
            _resident((1, D_MODEL)),
        ],
        out_specs=pl.BlockSpec((tm, D_MODEL), lambda i: (i, 0)),
        out_shape=jax.ShapeDtypeStruct((n, D_MODEL), F32),
        compiler_params=_params(("parallel",)),
        name="proj_ln",
    )(h, pre, w_out, g, b)


def _cmlp_kernel(h_ref, win_ref, lng_ref, lnb_ref, wsp_ref, bias_ref, wout_ref, g_ref, b_ref,
                 o_ref, *rest):
    *v_refs, pre_ref = rest
    tm = h_ref.shape[0]
    x = h_ref[...]
    uv = jnp.dot(x.astype(BF16), win_ref[...], preferred_element_type=F32)
    uv = 0.5 * uv * (1.0 + lax.erf(uv * math.sqrt(0.5)))
    u = uv[:, :CMLP_DV]
    v = _layer_norm(uv[:, CMLP_DV:], lng_ref[...], lnb_ref[...])
    for v_ref in v_refs:
        v_ref[...] = v
    vb = v.astype(BF16)
    row = lax.broadcasted_iota(jnp.int32, (CMLP_CHUNK, CMLP_CHUNK), 0)
    col = lax.broadcasted_iota(jnp.int32, (CMLP_CHUNK, CMLP_CHUNK), 1)
    causal = col <= row
    for g in range(CMLP_GROUPS):
        gl = g * LANES
        wg = jnp.where(causal, wsp_ref[g], 0.0).astype(BF16)
        n_chunks = tm // CMLP_CHUNK
        v_chunks = jnp.concatenate(
            [vb[c * CMLP_CHUNK:(c + 1) * CMLP_CHUNK, gl:gl + LANES] for c in range(n_chunks)], axis=1)
        s_chunks = jnp.dot(wg, v_chunks, preferred_element_type=F32)
        for c in range(n_chunks):
            cl = c * CMLP_CHUNK
            s = s_chunks[:, c * LANES:(c + 1) * LANES] + bias_ref[:, gl:gl + LANES]
            pre_ref[cl:cl + CMLP_CHUNK, gl:gl + LANES] = (u[cl:cl + CMLP_CHUNK, gl:gl + LANES] * s).astype(BF16)
    y = jnp.dot(pre_ref[...], wout_ref[...], preferred_element_type=F32)
    o_ref[...] = _layer_norm(DEEPNORM_ALPHA * x + y, g_ref[...], b_ref[...])


def _cmlp(h, w_in, ln_g, ln_b, w_sp, bias_full, w_out, g, b, *, emit_v, tm=512):
    n = h.shape[0]
    n_out = 2 if emit_v else 1
    return pl.pallas_call(
        _cmlp_kernel,
        grid=(n // tm,),
        in_specs=[
            pl.BlockSpec((tm, D_MODEL), lambda i: (i, 0)),
            _resident((D_MODEL, 2 * CMLP_DV)),
            _resident((1, CMLP_DV)),
            _resident((1, CMLP_DV)),
            _resident((CMLP_GROUPS, CMLP_CHUNK, CMLP_CHUNK)),
            _resident((CMLP_CHUNK, CMLP_DV)),
            _resident((CMLP_DV, D_MODEL)),
            _resident((1, D_MODEL)),
            _resident((1, D_MODEL)),
        ],
        out_specs=[pl.BlockSpec((tm, D_MODEL), lambda i: (i, 0))] * n_out,
        out_shape=[jax.ShapeDtypeStruct((n, D_MODEL), F32)] * n_out,
        scratch_shapes=[pltpu.VMEM((tm, CMLP_DV), BF16)],
        compiler_params=_params(("parallel",)),
        name="cmlp",
    )(h, w_in, ln_g, ln_b, w_sp, bias_full, w_out, g, b)


def _qkv_kernel(h_ref, w_ref, cos_ref, sin_ref, q_ref, k_ref, v_ref, *attn_refs):
    tm = h_ref.shape[0]
    qkv = jnp.dot(h_ref[...].astype(BF16), w_ref[...], preferred_element_type=F32)
    cos = cos_ref[...]
    sin = sin_ref[...]
    for hd in range(HEADS):
        lo = hd * HEAD_DIM
        qh = qkv[:, lo:lo + HEAD_DIM]
        q_ref[:, lo:lo + HEAD_DIM] = (qh * cos + pltpu.roll(qh, HEAD_DIM // 2, axis=1) * sin).astype(BF16)
        kh = qkv[:, D_MODEL + lo:D_MODEL + lo + HEAD_DIM]
        k_ref[:, lo:lo + HEAD_DIM] = kh * cos + pltpu.roll(kh, HEAD_DIM // 2, axis=1) * sin
    v = qkv[:, 2 * D_MODEL:]
    v_ref[...] = v
    if attn_refs:
        kb_ref, vt_ref, km_ref = attn_refs
        kb_ref[...] = k_ref[...].astype(BF16)
        vt_ref[...] = v.T.astype(BF16)
        for j in range(tm // MOBA_BLOCK):
            blk = k_ref[j * MOBA_BLOCK:(j + 1) * MOBA_BLOCK, :]
            km_ref[0, j:j + 1, :] = jnp.sum(blk, axis=0, keepdims=True) * (1.0 / MOBA_BLOCK)


def _qkv(h, w_qkv, cos, sin, *, for_prompt, tm=512):
    n = h.shape[0]
    nkb = tm // MOBA_BLOCK
    row_spec = lambda width: pl.BlockSpec((tm, width), lambda i: (i, 0))
    out_specs = [row_spec(D_MODEL)] * 3
    out_shape = [
        jax.ShapeDtypeStruct((n, D_MODEL), BF16),
        jax.ShapeDtypeStruct((n, D_MODEL), F32),
        jax.ShapeDtypeStruct((n, D_MODEL), F32),
    ]
    if for_prompt:
        out_specs += [
            row_spec(D_MODEL),
            pl.BlockSpec((D_MODEL, tm), lambda i: (0, i)),
            pl.BlockSpec((1, nkb, D_MODEL), lambda i: (i, 0, 0)),
        ]
        out_shape += [
            jax.ShapeDtypeStruct((n, D_MODEL), BF16),
            jax.ShapeDtypeStruct((D_MODEL, n), BF16),
            jax.ShapeDtypeStruct((n // tm, nkb, D_MODEL), F32),
        ]
    return pl.pallas_call(
        _qkv_kernel,
        grid=(n // tm,),
        in_specs=[
            row_spec(D_MODEL),
            _resident((D_MODEL, 3 * D_MODEL)),
            row_spec(HEAD_DIM),
            row_spec(HEAD_DIM),
        ],
        out_specs=out_specs,
        out_shape=out_shape,
        compiler_params=_params(("parallel",)),
        name="qkv_rope",
    )(h, w_qkv, cos, sin)


def _top3_sublanes(gate, n_valid):
    n_rows = gate.shape[0]
    blk = lax.broadcasted_iota(jnp.int32, gate.shape, 0)
    gate = jnp.where(blk < n_valid, gate, NEG_INF)
    blk = blk.astype(F32)
    bias = jnp.full(gate.shape, NEG_INF, F32)
    for _ in range(MOBA_TOPK):
        m = jnp.max(gate, axis=0, keepdims=True)
        first = jnp.min(jnp.where(gate == m, blk, float(n_rows)), axis=0, keepdims=True)
        pick = blk == first
        bias = jnp.where(pick & (m > NEG_INF), 0.0, bias)
        gate = jnp.where(pick, NEG_INF, gate)
    return bias


def _top3_lanes(gate, n_valid):
    lane = lax.broadcasted_iota(jnp.int32, gate.shape, 1)
    gate = jnp.where(lane < n_valid, gate, NEG_INF)
    bias = jnp.full(gate.shape, NEG_INF, F32)
    for _ in range(MOBA_TOPK):
        m = jnp.max(gate, axis=1, keepdims=True)
        first = jnp.min(jnp.where(gate == m, lane, LANES), axis=1, keepdims=True)
        pick = lane == first
        bias = jnp.where(pick & (m > NEG_INF), 0.0, bias)
        gate = jnp.where(pick, NEG_INF, gate)
    return bias


def _attn_prompt_kernel(q_ref, k_ref, vt_ref, km_ref, o_ref, bias_ref, acc_ref):
    i = pl.program_id(1)
    tq = q_ref.shape[0]
    c_exp = HEAD_DIM ** -0.5 * math.log2(math.e)
    nt = (((1,), (1,)), ((), ()))

    def scores(h, start):
        lo = h * HEAD_DIM
        kj = k_ref[0, pl.ds(start, MOBA_BLOCK), lo:lo + HEAD_DIM]
        return lax.dot_general(kj, q_ref[:, lo:lo + HEAD_DIM], nt, preferred_element_type=F32)

    def weighted_values(h, start, p):
        lo = h * HEAD_DIM
        return jnp.dot(vt_ref[lo:lo + HEAD_DIM, pl.ds(start, MOBA_BLOCK)], p.astype(BF16),
                       preferred_element_type=F32)

    parts = tq // MOBA_BLOCK
    first_own = i * parts
    own_starts = [pl.multiple_of((first_own + r) * MOBA_BLOCK, MOBA_BLOCK) for r in range(parts)]
    n_selectable = first_own + lax.broadcasted_iota(jnp.int32, (1, tq), 1) // MOBA_BLOCK
    key = lax.broadcasted_iota(jnp.int32, (MOBA_BLOCK, MOBA_BLOCK), 0)
    qry = lax.broadcasted_iota(jnp.int32, (MOBA_BLOCK, MOBA_BLOCK), 1)
    gates = [
        lax.dot_general(km_ref[0, :, h * HEAD_DIM:(h + 1) * HEAD_DIM],
                        q_ref[:, h * HEAD_DIM:(h + 1) * HEAD_DIM].astype(F32), nt,
                        precision=HIGHEST, preferred_element_type=F32)
        for h in range(HEADS)
    ]

    def own_scores(h, r):
        lo = h * HEAD_DIM
        kj = k_ref[0, pl.ds(own_starts[r], MOBA_BLOCK), lo:lo + HEAD_DIM]
        qr = q_ref[r * MOBA_BLOCK:(r + 1) * MOBA_BLOCK, lo:lo + HEAD_DIM]
        return lax.dot_general(kj, qr, nt, preferred_element_type=F32)

    own_ss = [[own_scores(h, r) for r in range(parts)] for h in range(HEADS)]
    ms, ls, ps = [], [], []
    for h in range(HEADS):
        bias_ref[h] = _top3_sublanes(gates[h], n_selectable)
        m_parts, l_parts, p_parts = [], [], []
        for r in range(parts):
            s = jnp.where(key <= qry, own_ss[h][r], NEG_INF)
            m = jnp.max(s, axis=0, keepdims=True)
            p = jnp.exp2((s - m) * c_exp)
            m_parts.append(m)
            l_parts.append(jnp.sum(p, axis=0, keepdims=True))
            p_parts.append(p)
        ms.append(jnp.concatenate(m_parts, axis=1))
        ls.append(jnp.concatenate(l_parts, axis=1))
        ps.append(p_parts)
    for h in range(HEADS):
        acc_ref[h] = jnp.concatenate(
            [weighted_values(h, own_starts[r], ps[h][r]) for r in range(parts)], axis=1)

    def body(j, carry):
        ms, ls = carry
        start = pl.multiple_of(j * MOBA_BLOCK, MOBA_BLOCK)
        ss = [scores(h, start) for h in range(HEADS)]
        new_ms, new_ls, alphas, ps = [], [], [], []
        for h in range(HEADS):
            bias_row = bias_ref[h, pl.ds(j, 1), :]
            m_new = jnp.maximum(ms[h], jnp.max(ss[h], axis=0, keepdims=True) + bias_row)
            alpha = jnp.exp2((ms[h] - m_new) * c_exp)
            p = jnp.exp2((ss[h] - (m_new - bias_row)) * c_exp)
            new_ms.append(m_new)
            new_ls.append(alpha * ls[h] + jnp.sum(p, axis=0, keepdims=True))
            alphas.append(alpha)
            ps.append(p.astype(BF16))
        for h in range(HEADS):
            acc_ref[h] = alphas[h] * acc_ref[h] + weighted_values(h, start, ps[h])
        return tuple(new_ms), tuple(new_ls)

    _, ls = lax.fori_loop(0, first_own + parts - 1, body, (tuple(ms), tuple(ls)))
    for h in range(HEADS):
        o_ref[h * HEAD_DIM:(h + 1) * HEAD_DIM, :] = (acc_ref[h] / ls[h]).astype(BF16)


def _attn_prompt(q, k, vt, kmean):
    bn, t, _ = k.shape
    nb = t // MOBA_BLOCK
    tq = MOBA_QUERY_TILE
    nq = t // tq
    return pl.pallas_call(
        _attn_prompt_kernel,
        grid=(bn, nq),
        in_specs=[
            pl.BlockSpec((tq, D_MODEL), lambda b, i: (b * nq + i, 0)),
            pl.BlockSpec((1, t, D_MODEL), lambda b, i: (b, 0, 0), pipeline_mode=pl.Buffered(1)),
            pl.BlockSpec((D_MODEL, t), lambda b, i: (0, b), pipeline_mode=pl.Buffered(1)),
            pl.BlockSpec((1, nb, D_MODEL), lambda b, i: (b, 0, 0)),
        ],
        out_specs=pl.BlockSpec((D_MODEL, tq), lambda b, i: (0, b * nq + i)),
        out_shape=jax.ShapeDtypeStruct((D_MODEL, bn * t), BF16),
        scratch_shapes=[
            pltpu.VMEM((HEADS, nb, tq), F32),
            pltpu.VMEM((HEADS, HEAD_DIM, tq), F32),
        ],
        compiler_params=_params(("parallel", "arbitrary")),
        name="moba_prompt",
    )(q, k, vt, kmean)


def _attn_sample_kernel(pt_ref, q_ref, kn_ref, vn_ref, *rest, n_pages):
    kp_refs = rest[:n_pages]
    vp_refs = rest[n_pages:2 * n_pages]
    o_ref, kb_ref, vb_ref, s_ref = rest[2 * n_pages:]
    del pt_ref
    tq = q_ref.shape[1]
    past = n_pages * PAGE_SIZE
    n_past_blocks = past // MOBA_BLOCK
    n_new = 2 * SUBLANES
    c_exp = HEAD_DIM ** -0.5 * math.log2(math.e)

    q = q_ref[0].astype(F32)
    nr = HEADS * tq
    qt = jnp.concatenate([q] * HEADS, axis=0)
    rid = lax.broadcasted_iota(jnp.int32, (nr, D_MODEL), 0)
    lid = lax.broadcasted_iota(jnp.int32, (nr, D_MODEL), 1)
    qr = jnp.where(lid // HEAD_DIM == rid // tq, qt, 0.0).astype(BF16)

    nt = (((1,), (1,)), ((), ()))
    for p in range(n_pages):
        rows = slice(p * PAGE_SIZE, (p + 1) * PAGE_SIZE)
        for hd in range(HEADS):
            lo = hd * HEAD_DIM
            head_rows = pl.ds(hd, PAGE_SIZE, stride=HEADS)
            kb_ref[rows, lo:lo + HEAD_DIM] = kp_refs[p][0, head_rows, :].astype(BF16)
            vb_ref[rows, lo:lo + HEAD_DIM] = vp_refs[p][0, head_rows, :].astype(BF16)
        s_ref[:, rows] = lax.dot_general(qr, kb_ref[rows, :], nt, preferred_element_type=F32)
    zpad = jnp.zeros((n_new - tq, D_MODEL), F32)
    kb_ref[past:past + n_new, :] = jnp.concatenate([kn_ref[0], zpad], axis=0).astype(BF16)
    vb_ref[past:past + n_new, :] = jnp.concatenate([vn_ref[0], zpad], axis=0).astype(BF16)

    lane = lax.broadcasted_iota(jnp.int32, (nr, LANES), 1)
    gate = jnp.full((nr, LANES), NEG_INF, F32)
    for n in range(n_past_blocks):
        lo = n * MOBA_BLOCK
        block_mean = jnp.sum(s_ref[:, lo:lo + MOBA_BLOCK], axis=1, keepdims=True) * (1.0 / MOBA_BLOCK)
        gate = jnp.where(lane == n, block_mean, gate)
    bias = _top3_lanes(gate, n_past_blocks)

    s_new = lax.dot_general(qr, kb_ref[past:past + n_new, :], nt, preferred_element_type=F32)
    qi = lax.broadcasted_iota(jnp.int32, (nr, n_new), 0) % tq
    kr = lax.broadcasted_iota(jnp.int32, (nr, n_new), 1)
    s_new = jnp.where(kr <= qi, s_new, NEG_INF)
    m = jnp.max(s_new, axis=1, keepdims=True)
    for n in range(n_past_blocks):
        lo = n * MOBA_BLOCK
        m = jnp.maximum(m, jnp.max(s_ref[:, lo:lo + MOBA_BLOCK], axis=1, keepdims=True) + bias[:, n:n + 1])
    p_new = jnp.exp2((s_new - m) * c_exp)
    l = jnp.sum(p_new, axis=1, keepdims=True)
    for n in range(n_past_blocks):
        lo = n * MOBA_BLOCK
        pb = jnp.exp2((s_ref[:, lo:lo + MOBA_BLOCK] - (m - bias[:, n:n + 1])) * c_exp)
        s_ref[:, lo:lo + MOBA_BLOCK] = pb
        l = l + jnp.sum(pb, axis=1, keepdims=True)
    inv_l = 1.0 / l
    out = jnp.dot((s_ref[:, :past] * inv_l).astype(BF16), vb_ref[:past, :], preferred_element_type=F32)
    out = out + jnp.dot((p_new * inv_l).astype(BF16), vb_ref[past:past + n_new, :], preferred_element_type=F32)
    for hd in range(HEADS):
        lo = hd * HEAD_DIM
        o_ref[0, :, lo:lo + HEAD_DIM] = out[hd * tq:(hd + 1) * tq, lo:lo + HEAD_DIM].astype(BF16)


def _attn_sample(q, k_new, v_new, cache_k, cache_v, page_table):
    ns, tq, _ = q.shape
    n_pages = page_table.shape[1]
    past = n_pages * PAGE_SIZE
    n_new = 2 * SUBLANES
    assert tq == SUBLANES and HEADS * tq <= LANES and past % MOBA_BLOCK == 0

    def page_spec(p):
        return pl.BlockSpec((1, PAGE_SIZE * HEADS, HEAD_DIM), lambda s, pt: (pt[s * n_pages + p], 0, 0))

    seq_spec = pl.BlockSpec((1, tq, D_MODEL), lambda s, pt: (s, 0, 0))
    grid_spec = pltpu.PrefetchScalarGridSpec(
        num_scalar_prefetch=1,
        grid=(ns,),
        in_specs=[seq_spec, seq_spec, seq_spec]
        + [page_spec(p) for p in range(n_pages)]
        + [page_spec(p) for p in range(n_pages)],
        out_specs=seq_spec,
        scratch_shapes=[
            pltpu.VMEM((past + n_new, D_MODEL), BF16),
            pltpu.VMEM((past + n_new, D_MODEL), BF16),
            pltpu.VMEM((HEADS * tq, past), F32),
        ],
    )
    return pl.pallas_call(
        functools.partial(_attn_sample_kernel, n_pages=n_pages),
        grid_spec=grid_spec,
        out_shape=jax.ShapeDtypeStruct((ns, tq, D_MODEL), BF16),
        compiler_params=_params(("arbitrary",)),
        name="moba_sample",
    )(page_table.reshape(-1), q, k_new, v_new, *([cache_k] * n_pages), *([cache_v] * n_pages))


def _ssd_chunk(raw, z, dtr, first_chunk, side_jobs, conv0_ref, ssm0_ref, wconv_ref, bconv_ref, dtb_ref, alog_ref,
               dskip_ref, ng_ref, tri_ref, y_ref, convo_ref, state_ref, ext_ref, yacc_ref):
    t = raw.shape[0]
    L = tri_ref.shape[0]
    halo = SUBLANES

    if first_chunk is not None:
        @pl.when(first_chunk)
        def _():
            ext_ref[0:halo, :] = conv0_ref[...]
            state_ref[...] = ssm0_ref[...]

    ext_ref[halo:halo + t, :] = raw
    if t < L:
        ext_ref[halo + t:halo + L, :] = jnp.zeros((L - t, SSM_CONV_DIM), F32)
    ext = ext_ref[...]
    conv = bconv_ref[...] + wconv_ref[SSM_CONV - 1:SSM_CONV, :] * ext[halo:, :]
    for back in range(1, SSM_CONV):
        w = SSM_CONV - 1 - back
        conv = conv + wconv_ref[w:w + 1, :] * pltpu.roll(ext, back, axis=0)[halo:, :]
    tail = ext_ref[t:t + halo, :]
    ext_ref[0:halo, :] = tail
    convo_ref[...] = tail
    xbc = _silu(conv)
    x = xbc[:, :SSM_D_INNER]
    bm = xbc[:, SSM_D_INNER:SSM_D_INNER + SSM_BC_DIM].astype(BF16)
    cm = xbc[:, SSM_D_INNER + SSM_BC_DIM:]

    dtx = dtr + dtb_ref[...]
    dt = jnp.maximum(dtx, 0.0) + jnp.log(1.0 + jnp.exp(-jnp.abs(dtx)))
    if t < L:
        dt = jnp.concatenate([dt, jnp.zeros((L - t, LANES), F32)], axis=0)
    a = -jnp.exp(alog_ref[...])
    adt = dt * a
    acs = jnp.dot(tri_ref[...], adt, precision=HIGHEST, preferred_element_type=F32)
    acs_t = acs.T
    dt_t = dt.T
    w_t = jnp.exp(acs_t[:, L - 1:L] - acs_t) * dt_t
    x_t = x.T
    xb = x.astype(BF16)

    row = lax.broadcasted_iota(jnp.int32, (L, L), 0)
    col = lax.broadcasted_iota(jnp.int32, (L, L), 1)
    causal = col <= row
    nt = (((1,), (1,)), ((), ()))
    for g in range(SSM_GROUPS):
        bg = bm[:, g * SSM_D_STATE:(g + 1) * SSM_D_STATE]
        cg = cm[:, g * SSM_D_STATE:(g + 1) * SSM_D_STATE]
        cb = lax.dot_general(cg.astype(BF16), bg, nt, preferred_element_type=F32)
        for e in range(SSM_HPG):
            hd = g * SSM_HPG + e
            lo = hd * SSM_HEAD_DIM
            acs_col = jnp.broadcast_to(acs[:, hd:hd + 1], (L, SSM_D_STATE))
            lmat = jnp.where(causal, jnp.exp(acs_col[:, :L] - acs_t[hd:hd + 1, :]), 0.0)
            mh = (cb * lmat * dt_t[hd:hd + 1, :]).astype(BF16)
            ch = (cg * jnp.exp(acs_col)).astype(BF16)
            sh = state_ref[lo:lo + SSM_HEAD_DIM, :]
            yh = jnp.dot(mh, xb[:, lo:lo + SSM_HEAD_DIM], preferred_element_type=F32)
            yh = yh + lax.dot_general(ch, sh.astype(BF16), nt, preferred_element_type=F32)
            yacc_ref[:, lo:lo + SSM_HEAD_DIM] = yh
            xw = (x_t[lo:lo + SSM_HEAD_DIM, :] * w_t[hd:hd + 1, :]).astype(BF16)
            upd = jnp.dot(xw, bg, preferred_element_type=F32)
            state_ref[lo:lo + SSM_HEAD_DIM, :] = jnp.exp(acs_t[hd:hd + 1, L - 1:L]) * sh + upd
            next(side_jobs, lambda: None)()
            yield

    y = yacc_ref[0:t, :] + x[:t] * dskip_ref[...]
    y = y * _silu(z)
    gw = SSM_D_INNER // SSM_GROUPS
    for g in range(SSM_GROUPS):
        yg = y[:, g * gw:(g + 1) * gw]
        yg = yg * lax.rsqrt(jnp.mean(yg * yg, axis=-1, keepdims=True) + RMS_EPS)
        y_ref[:, g * gw:(g + 1) * gw] = (yg * ng_ref[:, g * gw:(g + 1) * gw]).astype(y_ref.dtype)


SSM_PROJ_DIM = SSM_D_INNER + SSM_CONV_DIM + LANES
SSM_PROJ_COLS = 256


def _ssd_kernel(h_ref, hn_ref, conv0_ref, ssm0_ref, win_ref, *rest):
    *params, y_ref, convo_ref, ssmo_ref, ext_ref, yacc_ref, proj_ref, xn_ref = rest
    seqs, t, _ = h_ref.shape
    step = pl.program_id(0) * pl.num_programs(1) + pl.program_id(1)
    first_chunk = pl.program_id(1) == 0

    def project(x_ref, dst_ref):
        def piece(lo):
            hi = min(lo + SSM_PROJ_COLS, SSM_PROJ_DIM)

            def run():
                dst_ref[:, lo:hi] = jnp.dot(x_ref[...], win_ref[:, lo:hi], preferred_element_type=F32)
            return run
        return [piece(lo) for lo in range(0, SSM_PROJ_DIM, SSM_PROJ_COLS)]

    @pl.when(step == 0)
    def _():
        xn_ref[...] = h_ref[...].reshape(seqs * t, D_MODEL).astype(BF16)
        for job in project(xn_ref, proj_ref.at[0]):
            job()

    def run_step(cur_ref, nxt_ref):
        xn_ref[...] = hn_ref[...].reshape(seqs * t, D_MODEL).astype(BF16)
        side_jobs = iter(project(xn_ref, nxt_ref))
        chunk_rows = min(t, yacc_ref.shape[1])

        def sequence(s):
            for k in range(t // chunk_rows):
                rows = slice(s * t + k * chunk_rows, s * t + (k + 1) * chunk_rows)
                z = cur_ref[rows, :SSM_D_INNER]
                raw = cur_ref[rows, SSM_D_INNER:SSM_D_INNER + SSM_CONV_DIM]
                dtr = cur_ref[rows, SSM_D_INNER + SSM_CONV_DIM:]
                yield from _ssd_chunk(
                    raw, z, dtr, first_chunk if k == 0 else None, side_jobs, conv0_ref.at[s], ssm0_ref.at[s],
                    *params, y_ref.at[s, k * chunk_rows:(k + 1) * chunk_rows], convo_ref.at[s], ssmo_ref.at[s],
                    ext_ref.at[s], yacc_ref.at[s])

        active = [sequence(s) for s in range(seqs)]
        while active:
            active = [c for c in active if next(c, "done") != "done"]
        for job in side_jobs:
            job()

    for parity in range(2):
        @pl.when(step % 2 == parity)
        def _():
            run_step(proj_ref.at[parity], proj_ref.at[1 - parity])


def _ssd(h, conv0, ssm0, w_in, w_conv, b_conv, dt_bias, a_log, d_skip, norm_g, *, seqs_per_step):
    bn, t_total, _ = h.shape
    t = min(t_total, SSM_STEP_CHUNKS * SSM_CHUNK)
    nc = t_total // t
    chunk = max(min(t, SSM_CHUNK), 2 * SUBLANES)
    tri = jnp.tril(jnp.ones((chunk, chunk), F32))
    sq = seqs_per_step
    n_steps = (bn // sq) * nc
    blk = lambda width: pl.BlockSpec((sq, t, width), lambda b, c: (b, c, 0))
    per_seq = lambda rows, width: pl.BlockSpec((sq, rows, width), lambda b, c: (b, 0, 0))

    def next_step_rows(b, c):
        nxt = jnp.minimum(b * nc + c + 1, n_steps - 1)
        return nxt // nc, nxt % nc, 0

    return pl.pallas_call(
        _ssd_kernel,
        grid=(bn // sq, nc),
        in_specs=[
            blk(D_MODEL),
            pl.BlockSpec((sq, t, D_MODEL), next_step_rows),
            per_seq(SUBLANES, SSM_CONV_DIM),
            per_seq(SSM_D_INNER, SSM_D_STATE),
            _resident((D_MODEL, SSM_PROJ_DIM)),
            _resident((SUBLANES, SSM_CONV_DIM)),
            _resident((1, SSM_CONV_DIM)),
            _resident((1, LANES)),
            _resident((1, LANES)),
            _resident((1, SSM_D_INNER)),
            _resident((1, SSM_D_INNER)),
            _resident((chunk, chunk)),
        ],
        out_specs=[
            blk(SSM_D_INNER),
            per_seq(SUBLANES, SSM_CONV_DIM),
            per_seq(SSM_D_INNER, SSM_D_STATE),
        ],
        out_shape=[
            jax.ShapeDtypeStruct((bn, t_total, SSM_D_INNER), BF16),
            jax.ShapeDtypeStruct((bn, SUBLANES, SSM_CONV_DIM), F32),
            jax.ShapeDtypeStruct((bn, SSM_D_INNER, SSM_D_STATE), F32),
        ],
        scratch_shapes=[
            pltpu.VMEM((sq, SUBLANES + chunk, SSM_CONV_DIM), F32),
            pltpu.VMEM((sq, chunk, SSM_D_INNER), F32),
            pltpu.VMEM((2, sq * t, SSM_PROJ_DIM), F32),
            pltpu.VMEM((sq * t, D_MODEL), BF16),
        ],
        compiler_params=_params(("arbitrary", "arbitrary")),
        name="ssd",
    )(h, h, conv0, ssm0, w_in, w_conv, b_conv, dt_bias, a_log, d_skip, norm_g, tri)


def _rope_tables(pos):
    half = HEAD_DIM // 2
    inv = ROPE_THETA ** (-jnp.arange(half, dtype=F32) / half)
    ang = pos.astype(F32)[:, None] * inv[None, :]
    cos, sin = jnp.cos(ang), jnp.sin(ang)
    return jnp.concatenate([cos, cos], axis=-1), jnp.concatenate([-sin, sin], axis=-1)


def _row(v):
    return v.reshape(1, -1)


def kernel(x_prompt, x_sample, cache_k, cache_v, state_conv, state_ssm, page_table, ln_g, ln_b, ffn_w_in, ffn_w_out, cmlp_w_in, cmlp_ln_g, cmlp_ln_b, cmlp_w_s, cmlp_b_s, cmlp_w_out, moba_w_qkv, moba_w_out, ssm_w_in, ssm_w_conv, ssm_b_conv, ssm_dt_bias, ssm_a_log, ssm_d, ssm_norm_g, ssm_w_out):
    n_prompt, t_prompt, _ = x_prompt.shape
    n_sample, t_sample, _ = x_sample.shape
    n_pages = page_table.shape[1]
    past_len = n_pages * PAGE_SIZE
    hp = x_prompt.reshape(n_prompt * t_prompt, D_MODEL)
    hs = x_sample.reshape(n_sample * t_sample, D_MODEL)

    cos_p, sin_p = _rope_tables(jnp.arange(t_prompt))
    cos_p, sin_p = jnp.tile(cos_p, (n_prompt, 1)), jnp.tile(sin_p, (n_prompt, 1))
    cos_s, sin_s = _rope_tables(past_len + jnp.arange(t_sample))
    cos_s, sin_s = jnp.tile(cos_s, (n_sample, 1)), jnp.tile(sin_s, (n_sample, 1))

    outs = {k: [] for k in ("cmlp_v", "k_p", "v_p", "k_s", "v_s", "conv_p", "ssm_p", "conv_s", "ssm_s")}
    ffn_w_in_bf, ffn_w_out_bf = ffn_w_in.astype(BF16), ffn_w_out.astype(BF16)
    for i in range(DEPTH):
        hp, hs = _ffn(hp, hs, ffn_w_in_bf, ffn_w_out_bf, i, 0, _row(ln_g[i, 0]), _row(ln_b[i, 0]))
        kind, j = i % 3, i // 3
        g1, b1 = _row(ln_g[i, 1]), _row(ln_b[i, 1])
        if kind == 0:
            samples_per_chunk = CMLP_CHUNK // t_sample
            w_sp_s = jnp.einsum("ab,gts->gatbs", jnp.eye(samples_per_chunk, dtype=F32),
                                cmlp_w_s[j][:, :t_sample, :t_sample]).reshape(CMLP_GROUPS, CMLP_CHUNK, CMLP_CHUNK)
            bias_p = jnp.repeat(cmlp_b_s[j].T, CMLP_DV // CMLP_GROUPS, axis=1)
            bias_s = jnp.tile(jnp.repeat(cmlp_b_s[j][:, :t_sample].T, CMLP_DV // CMLP_GROUPS, axis=1),
                              (samples_per_chunk, 1))
            a_par = (cmlp_w_in[j].astype(BF16), _row(cmlp_ln_g[j]), _row(cmlp_ln_b[j]))
            w_o = cmlp_w_out[j].astype(BF16)
            (hp,) = _cmlp(hp, *a_par, cmlp_w_s[j], bias_p, w_o, g1, b1, emit_v=False)
            hs, v_rows = _cmlp(hs, *a_par, w_sp_s, bias_s, w_o, g1, b1, emit_v=True)
            outs["cmlp_v"].append(v_rows.reshape(n_sample, t_sample, CMLP_DV))
        elif kind == 1:
            w_qkv = moba_w_qkv[j].astype(BF16)
            w_o = moba_w_out[j].astype(BF16)
            q, k, v, kb, vt, kmean = _qkv(hp, w_qkv, cos_p, sin_p, for_prompt=True)
            o_t = _attn_prompt(q, kb.reshape(n_prompt, t_prompt, D_MODEL), vt,
                               kmean.reshape(n_prompt, t_prompt // MOBA_BLOCK, D_MODEL))
            hp = _proj_ln(hp, o_t, w_o, g1, b1, pre_transposed=True)
            outs["k_p"].append(k.reshape(n_prompt, t_prompt, HEADS, HEAD_DIM))
            outs["v_p"].append(v.reshape(n_prompt, t_prompt, HEADS, HEAD_DIM))
            q, k, v = _qkv(hs, w_qkv, cos_s, sin_s, for_prompt=False)
            shs = (n_sample, t_sample, D_MODEL)
            pool = cache_k.shape[1]
            o = _attn_sample(q.reshape(shs), k.reshape(shs), v.reshape(shs),
                             cache_k.reshape(-1, PAGE_SIZE * HEADS, HEAD_DIM),
                             cache_v.reshape(-1, PAGE_SIZE * HEADS, HEAD_DIM), page_table + j * pool)
            hs = _proj_ln(hs, o.reshape(-1, D_MODEL), w_o, g1, b1)
            outs["k_s"].append(k.reshape(n_sample, t_sample, HEADS, HEAD_DIM))
            outs["v_s"].append(v.reshape(n_sample, t_sample, HEADS, HEAD_DIM))
        else:
            c_par = (
                jnp.pad(ssm_w_in[j].astype(BF16), ((0, 0), (0, LANES - SSM_HEADS))),
                jnp.pad(ssm_w_conv[j], ((0, SUBLANES - SSM_CONV), (0, 0))),
                _row(ssm_b_conv[j]),
                _row(jnp.pad(ssm_dt_bias[j], (0, LANES - SSM_HEADS))),
                _row(jnp.pad(ssm_a_log[j], (0, LANES - SSM_HEADS))),
                _row(jnp.repeat(ssm_d[j], SSM_HEAD_DIM)),
                _row(ssm_norm_g[j]),
            )
            w_o = ssm_w_out[j].astype(BF16)
            halo_pad = ((0, 0), (SUBLANES - (SSM_CONV - 1), 0), (0, 0))

            y, conv_t, ssm_f = _ssd(
                hp.reshape(n_prompt, t_prompt, D_MODEL),
                jnp.zeros((n_prompt, SUBLANES, SSM_CONV_DIM), F32),
                jnp.zeros((n_prompt, SSM_D_INNER, SSM_D_STATE), F32), *c_par, seqs_per_step=1)
            hp = _proj_ln(hp, y.reshape(-1, SSM_D_INNER), w_o, g1, b1)
            outs["conv_p"].append(conv_t[:, SUBLANES - (SSM_CONV - 1):])
            outs["ssm_p"].append(ssm_f.reshape(n_prompt, SSM_HEADS, SSM_HEAD_DIM, SSM_D_STATE))

            y, conv_t, ssm_f = _ssd(
                hs.reshape(n_sample, t_sample, D_MODEL),
                jnp.pad(state_conv[j], halo_pad),
                state_ssm[j].reshape(n_sample, SSM_D_INNER, SSM_D_STATE), *c_par,
                seqs_per_step=SSD_SAMPLE_SEQS)
            hs = _proj_ln(hs, y.reshape(-1, SSM_D_INNER), w_o, g1, b1)
            outs["conv_s"].append(conv_t[:, SUBLANES - (SSM_CONV - 1):])
            outs["ssm_s"].append(ssm_f.reshape(n_sample, SSM_HEADS, SSM_HEAD_DIM, SSM_D_STATE))
        hp, hs = _ffn(hp, hs, ffn_w_in_bf, ffn_w_out_bf, i, 1, _row(ln_g[i, 2]), _row(ln_b[i, 2]))
    st = lambda name: jnp.stack(outs[name])
    return (hp.reshape(n_prompt, t_prompt, D_MODEL), hs.reshape(n_sample, t_sample, D_MODEL),
            st("cmlp_v"), st("k_p"), st("v_p"), st("k_s"), st("v_s"),
            st("conv_p"), st("ssm_p"), st("conv_s"), st("ssm_s"))
```
